```python
import jax, jax.numpy as jnp
from jax import lax
import numpy as np


D_MODEL = 1024
BATCH = 16
SEQ = 2048
DEPTH = 1

RWKV_HEADS = 8
RWKV_HEAD_DIM = 64
RWKV_WIDTH = RWKV_HEADS * RWKV_HEAD_DIM
DECAY_LORA = 64
AAA_LORA = 64
GATE_LORA = 128
RWKV_COLS = 3 * RWKV_WIDTH + DECAY_LORA + AAA_LORA + GATE_LORA
GN_EPS = 64e-5
CONV_WIDTH = 512
CONV_GROUPS = 8
CONV_K = 31
CONV_COLS = 2 * CONV_WIDTH
GATE_COLS = 2 * D_MODEL
IN_COLS = RWKV_COLS + CONV_COLS + GATE_COLS
LN_EPS = 1e-5
RMS_EPS = 1e-6
PEER_HEADS = 8
N_KEYS = 128
N_EXPERTS = N_KEYS * N_KEYS
PEER_QDIM = 128
PEER_HALF = PEER_QDIM // 2
PEER_TOPK = 16
PEER_CHUNK = 128
PLE_DIM = 256

kernel_name = 'hybrid_rwkv7_conformer_peer'


def rms_norm(x, g):
    xf = x.astype(jnp.float32)
    y = xf * lax.rsqrt(jnp.mean(xf * xf, axis=-1, keepdims=True) + RMS_EPS)
    return (y * g.astype(jnp.float32)).astype(x.dtype)


def token_shift(z):
    return jnp.pad(z, ((0, 0), (1, 0), (0, 0)))[:, :-1]


def rwkv7_recurrence(r, w, k, v, kk, a):
    B, T, H, N = r.shape

    def step(S, inp):
        r_t, w_t, k_t, v_t, kk_t, a_t = inp
        sa = jnp.einsum('bhvk,bhk->bhv', S, -kk_t)
        S = (S * w_t[:, :, None, :] + sa[..., None] * (kk_t * a_t)[:, :, None, :]
             + v_t[..., None] * k_t[:, :, None, :])
        return S, jnp.einsum('bhvk,bhk->bhv', S, r_t)

    xs = tuple(jnp.moveaxis(t, 1, 0) for t in (r, w, k, v, kk, a))
    S0 = jnp.zeros((B, H, N, N), jnp.float32)
    _, y = lax.scan(step, S0, xs)
    return jnp.moveaxis(y, 0, 1)


def rwkv7_branch(z, mu, w0, w_up, a0, a_up, g_up, k_k, k_a, r_k, lnx_g, lnx_b, w_o):
    B, T, _ = z.shape
    f32 = jnp.float32
    z = z + (token_shift(z) - z) * mu
    s0 = RWKV_WIDTH
    splits = [s0, 2 * s0, 3 * s0, 3 * s0 + DECAY_LORA, 3 * s0 + DECAY_LORA + AAA_LORA]
    r, k, v, wl, al, gl = jnp.split(z, splits, axis=-1)
    w_log = -jax.nn.softplus(-(w0 + jnp.tanh(wl) @ w_up)) - 0.5
    decay = jnp.exp(-jnp.exp(w_log.astype(f32)))
    a = jax.nn.sigmoid(a0 + al @ a_up)
    g = jax.nn.sigmoid(gl) @ g_up
    hd = lambda t: t.astype(f32).reshape(B, T, RWKV_HEADS, RWKV_HEAD_DIM)
    kk = hd(k * k_k)
    kk = kk * lax.rsqrt(jnp.maximum(jnp.sum(kk * kk, -1, keepdims=True), 1e-24))
    k = k * (1.0 + (a - 1.0) * k_a)
    rh, kh, vh, ah = hd(r), hd(k), hd(v), hd(a)
    y = rwkv7_recurrence(rh, hd(decay), kh, vh, kk, ah)
    mean = jnp.mean(y, -1, keepdims=True)
    var = jnp.mean(jnp.square(y - mean), -1, keepdims=True)
    y = ((y - mean) * lax.rsqrt(var + GN_EPS)).reshape(B, T, RWKV_WIDTH)
    y = y * lnx_g.astype(f32) + lnx_b.astype(f32)
    bonus = jnp.sum(rh * kh * r_k.astype(f32), -1, keepdims=True) * vh
    y = (y + bonus.reshape(B, T, RWKV_WIDTH)) * g.astype(f32)
    return y.astype(z.dtype) @ w_o


def conformer_conv_branch(z, dw_w, dw_b, ln_g, ln_b, w_o):
    u, gate = jnp.split(z, 2, axis=-1)
    h = u * jax.nn.sigmoid(gate)
    hp = jnp.pad(h, ((0, 0), (CONV_K - 1, 0), (0, 0)))
    h = lax.conv_general_dilated(hp, dw_w[:, None, :].astype(h.dtype), window_strides=(1,), padding='VALID',
                                 dimension_numbers=('NWC', 'WIO', 'NWC'),
                                 feature_group_count=CONV_WIDTH) + dw_b
    hf = h.astype(jnp.float32)
    mean = jnp.mean(hf, -1, keepdims=True)
    var = jnp.mean(jnp.square(hf - mean), -1, keepdims=True)
    hf = (hf - mean) * lax.rsqrt(var + LN_EPS) * ln_g.astype(jnp.float32) + ln_b.astype(jnp.float32)
    h = jax.nn.silu(hf).astype(z.dtype)
    return h @ w_o


def peer_ffn(c, w_q, sub_keys, u_tab, v_tab):
    B, T, D = c.shape
    q = (c @ w_q).reshape(B, T, PEER_HEADS, 2, PEER_HALF)
    s = jnp.einsum('bthpd,hpnd->bthpn', q, sub_keys).astype(jnp.float32)
    s1, i1 = lax.top_k(s[..., 0, :], PEER_TOPK)
    s2, i2 = lax.top_k(s[..., 1, :], PEER_TOPK)
    n_cand = PEER_TOPK * PEER_TOPK
    cand_s = (s1[..., :, None] + s2[..., None, :]).reshape(B, T, PEER_HEADS, n_cand)
    cand_i = (i1[..., :, None] * N_KEYS + i2[..., None, :]).reshape(B, T, PEER_HEADS, n_cand)
    best_s, pos = lax.top_k(cand_s, PEER_TOPK)
    idx = jnp.take_along_axis(cand_i, pos, axis=-1)
    gates = jax.nn.softmax(best_s, axis=-1)
    n_tok = B * T
    n_sel = PEER_HEADS * PEER_TOPK
    n_blk = n_tok // PEER_CHUNK
    xs = c.reshape(n_blk, PEER_CHUNK, D)
    idx = idx.reshape(n_blk, PEER_CHUNK, n_sel)
    gates = gates.reshape(n_blk, PEER_CHUNK, n_sel).astype(c.dtype)

    def block(args):
        xb, ib, gb = args
        u = jnp.take(u_tab, ib, axis=0)
        coef = gb * jax.nn.gelu(jnp.einsum('cd,ced->ce', xb, u), approximate=False)
        return jnp.einsum('ce,ced->cd', coef, jnp.take(v_tab, ib, axis=0))

    return lax.map(block, (xs, idx, gates)).reshape(B, T, D)


def setup_inputs(seed: int = 0) -> dict:
    key = jax.random.key(seed)
    ks = jax.random.split(key, 32)
    L, D = DEPTH, D_MODEL
    n = lambda k, shape, scale: jax.random.normal(k, shape, jnp.float32) * scale
    gain = lambda k, shape: 1.0 + 0.05 * jax.random.normal(k, shape, jnp.float32)
    return {
        'x': n(ks[0], (BATCH, SEQ, D), 1.0),
        'p': n(ks[1], (DEPTH, BATCH, SEQ, PLE_DIM), 1.0),
        'norm_mix_g': gain(ks[2], (L, D)),
        'w_in': n(ks[3], (L, D, IN_COLS), D ** -0.5),
        'rwkv_mu': jax.random.uniform(ks[4], (L, RWKV_COLS), jnp.float32),
        'rwkv_w0': jax.random.uniform(ks[5], (L, RWKV_WIDTH), jnp.float32, -6.0, 0.0),
        'rwkv_w_up': n(ks[6], (L, DECAY_LORA, RWKV_WIDTH), 0.1),
        'rwkv_a0': n(ks[7], (L, RWKV_WIDTH), 0.5),
        'rwkv_a_up': n(ks[8], (L, AAA_LORA, RWKV_WIDTH), 0.5 * AAA_LORA ** -0.5),
        'rwkv_g_up': n(ks[9], (L, GATE_LORA, RWKV_WIDTH), GATE_LORA ** -0.5),
        'rwkv_k_k': 0.85 + n(ks[10], (L, RWKV_WIDTH), 0.05),
        'rwkv_k_a': gain(ks[11], (L, RWKV_WIDTH)),
        'rwkv_r_k': n(ks[12], (L, RWKV_HEADS, RWKV_HEAD_DIM), 0.1),
        'rwkv_lnx_g': gain(ks[13], (L, RWKV_WIDTH)),
        'rwkv_lnx_b': n(ks[14], (L, RWKV_WIDTH), 0.01),
        'rwkv_w_o': n(ks[15], (L, RWKV_WIDTH, D), RWKV_WIDTH ** -0.5),
        'conv_dw_w': n(ks[16], (L, CONV_K, CONV_WIDTH), CONV_K ** -0.5),
        'conv_dw_b': n(ks[17], (L, CONV_WIDTH), 0.01),
        'conv_ln_g': gain(ks[18], (L, CONV_WIDTH)),
        'conv_ln_b': n(ks[19], (L, CONV_WIDTH), 0.01),
        'conv_w_o': n(ks[20], (L, CONV_WIDTH, D), CONV_WIDTH ** -0.5),
        'w_out': n(ks[21], (L, D, D), D ** -0.5),
        'norm_ffn_g': gain(ks[22], (L, D)),
        'peer_w_q': n(ks[23], (L, D, PEER_HEADS * PEER_QDIM), D ** -0.5),
        'peer_sub_keys': n(ks[24], (L, PEER_HEADS, 2, N_KEYS, PEER_HALF), PEER_HALF ** -0.5),
        'peer_u': n(ks[25], (L, N_EXPERTS, D), D ** -0.5),
        'peer_v': n(ks[26], (L, N_EXPERTS, D), 0.5),
        'ple_w_proj': n(ks[27], (L, PLE_DIM, D), PLE_DIM ** -0.5),
        'ple_norm_g': gain(ks[28], (L, D)),
        'ple_w_gate': n(ks[29], (L, D, D), D ** -0.5),
        'final_norm_g': gain(ks[30], (D,)),
    }


def reference(x, p, norm_mix_g, w_in, rwkv_mu, rwkv_w0, rwkv_w_up, rwkv_a0, rwkv_a_up, rwkv_g_up,
              rwkv_k_k, rwkv_k_a, rwkv_r_k, rwkv_lnx_g, rwkv_lnx_b, rwkv_w_o, conv_dw_w, conv_dw_b,
              conv_ln_g, conv_ln_b, conv_w_o, w_out, norm_ffn_g, peer_w_q, peer_sub_keys, peer_u, peer_v,
              ple_w_proj, ple_norm_g, ple_w_gate, final_norm_g):
    h = x
    for i in range(DEPTH):
        a = rms_norm(h, norm_mix_g[i])
        z = a @ w_in[i]
        z_rwkv, z_conv, z_gate = jnp.split(z, [RWKV_COLS, RWKV_COLS + CONV_COLS], axis=-1)
        o_r = rwkv7_branch(z_rwkv, rwkv_mu[i], rwkv_w0[i], rwkv_w_up[i], rwkv_a0[i], rwkv_a_up[i],
                           rwkv_g_up[i], rwkv_k_k[i], rwkv_k_a[i], rwkv_r_k[i], rwkv_lnx_g[i],
                           rwkv_lnx_b[i], rwkv_w_o[i])
        o_c = conformer_conv_branch(z_conv, conv_dw_w[i], conv_dw_b[i], conv_ln_g[i], conv_ln_b[i], conv_w_o[i])
        g_r, g_c = jnp.split(jax.nn.sigmoid(z_gate), 2, axis=-1)
        h = h + (g_r * o_r + g_c * o_c) @ w_out[i]
        h = h + peer_ffn(rms_norm(h, norm_ffn_g[i]), peer_w_q[i], peer_sub_keys[i], peer_u[i], peer_v[i])
        ple_gate = jax.nn.sigmoid(rms_norm(h, ple_norm_g[i]) @ ple_w_gate[i])
        h = h + ple_gate * (p[i] @ ple_w_proj[i])
    return rms_norm(h, final_norm_g)
```

```python
import functools

import numpy as np
import jax
import jax.numpy as jnp
from jax import lax
from jax.experimental import pallas as pl
from jax.experimental.pallas import tpu as pltpu

F32 = jnp.float32
BF16 = jnp.bfloat16

RWKV_HEADS = 8
RWKV_HEAD_DIM = 64
RWKV_WIDTH = RWKV_HEADS * RWKV_HEAD_DIM
DECAY_LORA = 64
AAA_LORA = 64
GATE_LORA = 128
RWKV_COLS = 3 * RWKV_WIDTH + DECAY_LORA + AAA_LORA + GATE_LORA
GN_EPS = 64e-5
CONV_WIDTH = 512
CONV_K = 31
CONV_HALO = 32
LN_EPS = 1e-5
RMS_EPS = 1e-6
PEER_HEADS = 8
N_KEYS = 128
PEER_HALF = 64
PEER_TOPK = 16

VMEM_LIMIT = 56 * 1024 * 1024


def _cparams(sem):
    return pltpu.CompilerParams(dimension_semantics=sem, vmem_limit_bytes=VMEM_LIMIT)


def _dot(a, b):
    return jnp.dot(a, b, preferred_element_type=F32)


def _dot_nt(a, b):
    return lax.dot_general(a, b, (((1,), (1,)), ((), ())), preferred_element_type=F32)


def _dot_split(x, m_bf16):
    hi = x.astype(BF16)
    lo = (x - hi.astype(F32)).astype(BF16)
    return _dot(hi, m_bf16) + _dot(lo, m_bf16)


def _rms(x, g):
    return x * lax.rsqrt(jnp.mean(x * x, axis=-1, keepdims=True) + RMS_EPS) * g


def _in_proj_kernel(x_ref, g_ref, w_ref, zr_ref, hg_ref, sg_ref):
    a = _rms(x_ref[...], g_ref[...]).astype(BF16)
    c0, c1, c2 = RWKV_COLS, RWKV_COLS + CONV_WIDTH, RWKV_COLS + 2 * CONV_WIDTH
    zr_ref[...] = _dot(a, w_ref[:, :c0])
    u = _dot(a, w_ref[:, c0:c1])
    gate = _dot(a, w_ref[:, c1:c2])
    hg_ref[...] = u * jax.nn.sigmoid(gate)
    sg_ref[...] = jax.nn.sigmoid(_dot(a, w_ref[:, c2:])).astype(BF16)


def _in_proj(x, g, w_in, tm):
    n, d = x.shape
    cols = w_in.shape[1]
    gate_cols = cols - RWKV_COLS - 2 * CONV_WIDTH
    row = lambda w: pl.BlockSpec((tm, w), lambda i: (i, 0))
    full = lambda a: pl.BlockSpec(a.shape, lambda i: (0,) * a.ndim)
    return pl.pallas_call(
        _in_proj_kernel,
        grid=(n // tm,),
        in_specs=[row(d), full(g), full(w_in)],
        out_specs=[row(RWKV_COLS), row(CONV_WIDTH), row(gate_cols)],
        out_shape=[jax.ShapeDtypeStruct((n, RWKV_COLS), F32),
                   jax.ShapeDtypeStruct((n, CONV_WIDTH), F32),
                   jax.ShapeDtypeStruct((n, gate_cols), BF16)],
        compiler_params=_cparams(("arbitrary",)),
        name="in_proj",
    )(x, g, w_in)


def _rwkv_prep_kernel(tiles_per_seq, z_ref, mu_ref, w0_ref, wup_ref, a0_ref, aup_ref, gup_ref,
                      kk_ref, ka_ref, rk_ref, seg_ref,
                      r_out, w_out, k_out, v_out, nkk_out, kka_out, g_out, bonus_out, last_ref):
    i = pl.program_id(0)

    @pl.when(i % tiles_per_seq == 0)
    def _():
        last_ref[...] = jnp.zeros_like(last_ref)

    z = z_ref[...]
    tm = z.shape[0]
    rows = lax.broadcasted_iota(jnp.int32, z.shape, 0)
    zs = jnp.where(rows == 0, last_ref[...], pltpu.roll(z, 1, axis=0))
    last_ref[...] = z[tm - 1:tm, :]
    z = z + (zs - z) * mu_ref[...]

    s0 = RWKV_WIDTH
    r = z[:, :s0]
    k = z[:, s0:2 * s0]
    v = z[:, 2 * s0:3 * s0]
    o = 3 * s0
    wl = z[:, o:o + DECAY_LORA]
    al = z[:, o + DECAY_LORA:o + DECAY_LORA + AAA_LORA]
    gl = z[:, o + DECAY_LORA + AAA_LORA:]

    hp = lax.Precision.HIGHEST
    w_log = -jax.nn.softplus(-(w0_ref[...] + jnp.dot(jnp.tanh(wl), wup_ref[...], precision=hp))) - 0.5
    decay = jnp.exp(-jnp.exp(w_log))
    a = jax.nn.sigmoid(a0_ref[...] + jnp.dot(al, aup_ref[...], precision=hp))
    g = jnp.dot(jax.nn.sigmoid(gl), gup_ref[...], precision=hp)

    seg = seg_ref[...]
    kk = k * kk_ref[...]
    kk = kk * lax.rsqrt(jnp.maximum(_dot_split(kk * kk, seg), 1e-24))
    k2 = k * (1.0 + (a - 1.0) * ka_ref[...])
    bonus = _dot_split(r * k2 * rk_ref[...], seg) * v

    r_out[...] = r
    w_out[...] = decay
    k_out[...] = k2
    v_out[...] = v
    nkk_out[...] = -kk
    kka_out[...] = kk * a
    g_out[...] = g
    bonus_out[...] = bonus


def _rwkv_prep(zr, mu, w0, w_up, a0, a_up, g_up, k_k, k_a, r_k, seg, tm, seq):
    n = zr.shape[0]
    row = lambda w: pl.BlockSpec((tm, w), lambda i: (i, 0))
    full = lambda a: pl.BlockSpec(a.shape, lambda i: (0,) * a.ndim)
    consts = (mu, w0, w_up, a0, a_up, g_up, k_k, k_a, r_k, seg)
    return pl.pallas_call(
        functools.partial(_rwkv_prep_kernel, seq // tm),
        grid=(n // tm,),
        in_specs=[row(RWKV_COLS)] + [full(c) for c in consts],
        out_specs=[row(RWKV_WIDTH)] * 8,
        out_shape=[jax.ShapeDtypeStruct((n, RWKV_WIDTH), F32)] * 8,
        scratch_shapes=[pltpu.VMEM((1, RWKV_COLS), F32)],
        compiler_params=_cparams(("arbitrary",)),
        name="rwkv_prep",
    )(zr, *consts)


def _rwkv_scan_kernel(r_ref, w_ref, k_ref, v_ref, kka_ref, nkkn_ref, y_ref, s_ref, sa_ref):
    @pl.when(pl.program_id(0) == 0)
    def _():
        s_ref[...] = jnp.zeros_like(s_ref)
        sa_ref[...] = jnp.zeros_like(sa_ref)

    steps, hd, _ = r_ref.shape

    def step(t, carry):
        sa_old = sa_ref[...]
        v_t = v_ref[t]
        y = [jnp.zeros_like(sa_old), jnp.zeros_like(sa_old)]
        sa = [jnp.zeros_like(sa_old), jnp.zeros_like(sa_old)]
        for kx in range(hd):
            row = lambda ref: ref[t, pl.ds(kx, 1), :]
            s_k = s_ref[kx] * row(w_ref) + sa_old * row(kka_ref) + v_t * row(k_ref)
            s_ref[kx] = s_k
            y[kx % 2] = y[kx % 2] + s_k * row(r_ref)
            sa[kx % 2] = sa[kx % 2] + s_k * row(nkkn_ref)
        y_ref[t] = y[0] + y[1]
        sa_ref[...] = sa[0] + sa[1]
        return carry

    lax.fori_loop(0, steps, step, 0)


def _rwkv_scan(r, w, k, v, kka, nkkn, steps):
    t, hd, chains = r.shape
    blk = pl.BlockSpec((steps, hd, chains), lambda i: (i, 0, 0))
    return pl.pallas_call(
        _rwkv_scan_kernel,
        grid=(t // steps,),
        in_specs=[blk] * 6,
        out_specs=blk,
        out_shape=jax.ShapeDtypeStruct((t, hd, chains), F32),
        scratch_shapes=[pltpu.VMEM((hd, hd, chains), F32), pltpu.VMEM((hd, chains), F32)],
        compiler_params=_cparams(("arbitrary",)),
        name="rwkv_scan",
    )(r, w, k, v, kka, nkkn)


def _mix_out_kernel(tiles_per_seq, x_ref, y_ref, g_ref, bonus_ref, hg_ref, sg_ref, seg_ref,
                    lnxg_ref, lnxb_ref, wor_ref, dww_ref, dwb_ref, clng_ref, clnb_ref, woc_ref,
                    wout_ref, nfg_ref, wq_ref, h1_out, c_out, q_out, buf_ref):
    i = pl.program_id(0)
    tm = x_ref.shape[0]

    @pl.when(i % tiles_per_seq == 0)
    def _():
        buf_ref[:CONV_HALO, :] = jnp.zeros((CONV_HALO, CONV_WIDTH), F32)

    seg = seg_ref[...]
    y = y_ref[...]
    inv_n = 1.0 / RWKV_HEAD_DIM
    mean = _dot_split(y, seg) * inv_n
    yc = y - mean
    var = _dot_split(yc * yc, seg) * inv_n
    yn = yc * lax.rsqrt(var + GN_EPS) * lnxg_ref[...] + lnxb_ref[...]
    o_r = _dot(((yn + bonus_ref[...]) * g_ref[...]).astype(BF16), wor_ref[...])

    buf_ref[CONV_HALO:, :] = hg_ref[...]
    acc = jnp.zeros((tm, CONV_WIDTH), F32) + dwb_ref[...]
    for j in range(CONV_K):
        off = CONV_HALO - (CONV_K - 1) + j
        acc = acc + buf_ref[off:off + tm, :] * dww_ref[j:j + 1, :]
    buf_ref[:CONV_HALO, :] = buf_ref[tm:tm + CONV_HALO, :]
    mu = jnp.mean(acc, axis=-1, keepdims=True)
    ac = acc - mu
    cv = jnp.mean(ac * ac, axis=-1, keepdims=True)
    hf = ac * lax.rsqrt(cv + LN_EPS) * clng_ref[...] + clnb_ref[...]
    o_c = _dot((hf * jax.nn.sigmoid(hf)).astype(BF16), woc_ref[...])

    d = x_ref.shape[1]
    sg = sg_ref[...].astype(F32)
    mix = sg[:, :d] * o_r + sg[:, d:] * o_c
    h1 = x_ref[...] + _dot(mix.astype(BF16), wout_ref[...])
    h1_out[...] = h1
    c = _rms(h1, nfg_ref[...]).astype(BF16)
    c_out[...] = c
    q_out[...] = _dot(c, wq_ref[...])


def _mix_out(x, y, g, bonus, hg, sg, seg, consts, tm, seq):
    n, d = x.shape
    row = lambda w: pl.BlockSpec((tm, w), lambda i: (i, 0))
    full = lambda a: pl.BlockSpec(a.shape, lambda i: (0,) * a.ndim)
    return pl.pallas_call(
        functools.partial(_mix_out_kernel, seq // tm),
        grid=(n // tm,),
        in_specs=[row(d), row(RWKV_WIDTH), row(RWKV_WIDTH), row(RWKV_WIDTH), row(CONV_WIDTH), row(2 * d),
                  full(seg)] + [full(c) for c in consts],
        out_specs=[row(d), row(d), row(d)],
        out_shape=[jax.ShapeDtypeStruct((n, d), F32), jax.ShapeDtypeStruct((n, d), BF16),
                   jax.ShapeDtypeStruct((n, d), F32)],
        scratch_shapes=[pltpu.VMEM((tm + CONV_HALO, CONV_WIDTH), F32)],
        compiler_params=_cparams(("arbitrary",)),
        name="mix_out",
    )(x, y, g, bonus, hg, sg, seg, *consts)


def _sort16_pairs():
    n, pairs, p = PEER_TOPK, [], 1
    while p < n:
        k = p
        while k >= 1:
            for j in range(k % p, n - k, 2 * k):
                for i in range(min(k, n - j - k)):
                    if (i + j) // (2 * p) == (i + j + k) // (2 * p):
                        pairs.append((i + j, i + j + k))
            k //= 2
        p *= 2
    return pairs


_SORT16 = _sort16_pairs()
_BITONIC16 = [(i, i + dd) for dd in (8, 4, 2, 1) for i in range(PEER_TOPK) if not i & dd]
_CELLS = [(i, j) for i in range(PEER_TOPK) for j in range(PEER_TOPK) if (i + 1) * (j + 1) <= PEER_TOPK]


def _exchange(vals, pairs):
    for a, b in pairs:
        hi, lo = jnp.maximum(vals[a], vals[b]), jnp.minimum(vals[a], vals[b])
        vals[a], vals[b] = hi, lo
    return vals


def _top16_sorted(slabs):
    groups = []
    for s in range(0, len(slabs), PEER_TOPK):
        groups.append(_exchange(list(slabs[s:s + PEER_TOPK]), _SORT16))
    while len(groups) > 1:
        merged = []
        for a, b in zip(groups[::2], groups[1::2]):
            c = [jnp.maximum(a[i], b[PEER_TOPK - 1 - i]) for i in range(PEER_TOPK)]
            merged.append(_exchange(c, _BITONIC16))
        groups = merged
    return groups[0]


def _peer_route_kernel(q_ref, ska0_ref, ska1_ref, skb1_ref, theta_out, e1_out, s2_out, e2_out, s1_ref, s2_ref):
    qh = q_ref[...].astype(BF16)
    half = qh.shape[1] // 2
    s1_ref[...] = _dot_nt(ska0_ref[...], qh[:, :half])
    s2_ref[...] = _dot_nt(ska1_ref[...], qh[:, half:])
    nh = PEER_HEADS
    slab = lambda ref, n: ref[n * nh:(n + 1) * nh, :]

    v1 = _top16_sorted([slab(s1_ref, n) for n in range(N_KEYS)])
    v2 = _top16_sorted([slab(s2_ref, n) for n in range(N_KEYS)])

    cand = {c: v1[c[0]] + v2[c[1]] for c in _CELLS}
    cnt = {c: jnp.full_like(v1[0], float((c[0] + 1) * (c[1] + 1) - 1)) for c in _CELLS}
    for c1 in _CELLS:
        for c2 in _CELLS:
            if c1[0] < c2[0] and c1[1] > c2[1]:
                first = cand[c1] >= cand[c2]
                cnt[c2] = cnt[c2] + jnp.where(first, 1.0, 0.0)
                cnt[c1] = cnt[c1] + jnp.where(first, 0.0, 1.0)
    top = cand[(0, 0)]
    z = jnp.zeros_like(top)
    theta = [jnp.full_like(top, jnp.inf) for _ in range(PEER_TOPK)]
    for c in _CELLS:
        sel = cnt[c] < float(PEER_TOPK)
        z = z + jnp.where(sel, jnp.exp(cand[c] - top), 0.0)
        theta[c[0]] = jnp.where(sel, jnp.minimum(theta[c[0]], v2[c[1]]), theta[c[0]])
    inv_z = 1.0 / z

    for n in range(N_KEYS):
        s = slab(s1_ref, n)
        th = jnp.full_like(s, jnp.inf)
        for i in reversed(range(PEER_TOPK)):
            th = jnp.where(s == v1[i], theta[i], th)
        theta_out[n * nh:(n + 1) * nh, :] = th
        e1_out[n * nh:(n + 1) * nh, :] = jnp.exp(s - v1[0]) * inv_z

    s2b = _dot_nt(skb1_ref[...], qh[:, half:])
    s2_out[...] = s2b
    for h in range(nh):
        e2_out[h * N_KEYS:(h + 1) * N_KEYS, :] = jnp.exp(s2b[h * N_KEYS:(h + 1) * N_KEYS, :] - v2[0][h:h + 1, :])


def _peer_route(q, ska0, ska1, skb1, tn):
    n, d = q.shape
    rows = PEER_HEADS * N_KEYS
    full = lambda a: pl.BlockSpec(a.shape, lambda i: (0,) * a.ndim)
    col = pl.BlockSpec((rows, tn), lambda i: (0, i))
    return pl.pallas_call(
        _peer_route_kernel,
        grid=(n // tn,),
        in_specs=[pl.BlockSpec((tn, d), lambda i: (i, 0)), full(ska0), full(ska1), full(skb1)],
        out_specs=[col] * 4,
        out_shape=[jax.ShapeDtypeStruct((rows, n), F32)] * 4,
        scratch_shapes=[pltpu.VMEM((rows, tn), F32), pltpu.VMEM((rows, tn), F32)],
        compiler_params=_cparams(("arbitrary",)),
        name="peer_route",
    )(q, ska0, ska1, skb1)


def _peer_dense_kernel(c_ref, u_ref, vt_ref, theta_ref, e1_ref, s2_ref, e2_ref, out_ref, coef_ref):
    @pl.when(pl.program_id(1) == 0)
    def _():
        out_ref[...] = jnp.zeros_like(out_ref)

    nh = PEER_HEADS
    a_blk = u_ref.shape[0] // N_KEYS
    for al in range(a_blk):
        rows = slice(al * N_KEYS, (al + 1) * N_KEYS)
        ht = _dot_nt(u_ref[rows, :], c_ref[...])
        gate = jnp.zeros_like(ht)
        for h in range(nh):
            r = al * nh + h
            keep = s2_ref[h * N_KEYS:(h + 1) * N_KEYS, :] >= theta_ref[r:r + 1, :]
            gate = gate + jnp.where(keep, e2_ref[h * N_KEYS:(h + 1) * N_KEYS, :], 0.0) * e1_ref[r:r + 1, :]
        act = 0.5 * ht * (1.0 + lax.erf(ht * (2.0 ** -0.5)))
        coef_ref[rows, :] = (gate * act).astype(BF16)
    out_ref[...] += _dot(vt_ref[...], coef_ref[...])


def _peer_dense(c, u_bf, vt_bf, theta, e1, s2, e2, tn, a_blk):
    n, d = c.shape
    n_exp = u_bf.shape[0]
    e_blk = a_blk * N_KEYS
    rows = PEER_HEADS * N_KEYS
    return pl.pallas_call(
        _peer_dense_kernel,
        grid=(n // tn, n_exp // e_blk),
        in_specs=[pl.BlockSpec((tn, d), lambda i, j: (i, 0)),
                  pl.BlockSpec((e_blk, d), lambda i, j: (j, 0)),
                  pl.BlockSpec((d, e_blk), lambda i, j: (0, j)),
                  pl.BlockSpec((a_blk * PEER_HEADS, tn), lambda i, j: (j, i)),
                  pl.BlockSpec((a_blk * PEER_HEADS, tn), lambda i, j: (j, i)),
                  pl.BlockSpec((rows, tn), lambda i, j: (0, i)),
                  pl.BlockSpec((rows, tn), lambda i, j: (0, i))],
        out_specs=pl.BlockSpec((d, tn), lambda i, j: (0, i)),
        out_shape=jax.ShapeDtypeStruct((d, n), F32),
        scratch_shapes=[pltpu.VMEM((e_blk, tn), BF16)],
        compiler_params=_cparams(("arbitrary", "arbitrary")),
        name="peer_dense",
    )(c, u_bf, vt_bf, theta, e1, s2, e2)


def _final_kernel(last_layer, h1_ref, ffnt_ref, p_ref, png_ref, wg_ref, wp_ref, fng_ref, out_ref):
    h2 = h1_ref[...] + ffnt_ref[...].T
    gate = jax.nn.sigmoid(_dot(_rms(h2, png_ref[...]).astype(BF16), wg_ref[...]))
    h3 = h2 + gate * _dot(p_ref[...].astype(BF16), wp_ref[...])
    out_ref[...] = _rms(h3, fng_ref[...]) if last_layer else h3


def _final(h1, ffnt, p, png, wg, wp, fng, tm, last_layer):
    n, d = h1.shape
    full = lambda a: pl.BlockSpec(a.shape, lambda i: (0,) * a.ndim)
    return pl.pallas_call(
        functools.partial(_final_kernel, last_layer),
        grid=(n // tm,),
        in_specs=[pl.BlockSpec((tm, d), lambda i: (i, 0)), pl.BlockSpec((d, tm), lambda i: (0, i)),
                  pl.BlockSpec((tm, p.shape[1]), lambda i: (i, 0)), full(png), full(wg), full(wp), full(fng)],
        out_specs=pl.BlockSpec((tm, d), lambda i: (i, 0)),
        out_shape=jax.ShapeDtypeStruct((n, d), F32),
        compiler_params=_cparams(("arbitrary",)),
        name="final",
    )(h1, ffnt, p, png, wg, wp, fng)


def _tile(n, want):
    t = min(n, want)
    assert n % t == 0, (n, t)
    return t


def kernel(x, p, norm_mix_g, w_in, rwkv_mu, rwkv_w0, rwkv_w_up, rwkv_a0, rwkv_a_up, rwkv_g_up, rwkv_k_k, rwkv_k_a, rwkv_r_k, rwkv_lnx_g, rwkv_lnx_b, rwkv_w_o, conv_dw_w, conv_dw_b, conv_ln_g, conv_ln_b, conv_w_o, w_out, norm_ffn_g, peer_w_q, peer_sub_keys, peer_u, peer_v, ple_w_proj, ple_norm_g, ple_w_gate, final_norm_g):
    bsz, seq, d = x.shape
    n = bsz * seq
    nh, hd = RWKV_HEADS, RWKV_HEAD_DIM
    row = lambda a: a.reshape(1, -1)
    head_of = np.arange(RWKV_WIDTH) // hd
    seg = jnp.asarray(head_of[:, None] == head_of[None, :], BF16)
    h = x.reshape(n, d)

    for l in range(w_in.shape[0]):
        zr, hg, sg = _in_proj(h, row(norm_mix_g[l]), w_in[l].astype(BF16), _tile(seq, 512))
        r, w, k, v, nkk, kka, g, bonus = _rwkv_prep(
            zr, row(rwkv_mu[l]), row(rwkv_w0[l]), rwkv_w_up[l], row(rwkv_a0[l]), rwkv_a_up[l], rwkv_g_up[l],
            row(rwkv_k_k[l]), row(rwkv_k_a[l]), row(rwkv_r_k[l]), seg, _tile(seq, 512), seq)

        to_scan = lambda a: a.reshape(bsz, seq, nh, hd).transpose(1, 3, 0, 2).reshape(seq, hd, bsz * nh)
        nkk_s = to_scan(nkk)
        nkk_next = jnp.concatenate([nkk_s[1:], jnp.zeros_like(nkk_s[:1])], axis=0)
        y = _rwkv_scan(to_scan(r), to_scan(w), to_scan(k), to_scan(v), to_scan(kka), nkk_next, _tile(seq, 16))
        y = y.reshape(seq, hd, bsz, nh).transpose(2, 0, 3, 1).reshape(n, RWKV_WIDTH)

        consts = (row(rwkv_lnx_g[l]), row(rwkv_lnx_b[l]), rwkv_w_o[l].astype(BF16), conv_dw_w[l],
                  row(conv_dw_b[l]), row(conv_ln_g[l]), row(conv_ln_b[l]), conv_w_o[l].astype(BF16),
                  w_out[l].astype(BF16), row(norm_ffn_g[l]),
                  peer_w_q[l].reshape(d, PEER_HEADS, 2, PEER_HALF).transpose(0, 2, 1, 3).reshape(d, -1).astype(BF16))
        h1, c, q = _mix_out(h, y, g, bonus, hg, sg, seg, consts, _tile(seq, 256), seq)

        sk = peer_sub_keys[l]
        eye = jnp.eye(PEER_HEADS, dtype=sk.dtype)
        blk = lambda half, order: jnp.einsum("hnd,hg->" + order, sk[:, half], eye).reshape(
            PEER_HEADS * N_KEYS, PEER_HEADS * PEER_HALF).astype(BF16)
        theta, e1, s2, e2 = _peer_route(q, blk(0, "nhgd"), blk(1, "nhgd"), blk(1, "hngd"), _tile(n, 256))
        ffnt = _peer_dense(c, peer_u[l].astype(BF16), peer_v[l].T.astype(BF16), theta, e1, s2, e2,
                           _tile(n, 512), 8)
        h = _final(h1, ffnt, p[l].reshape(n, -1), row(ple_norm_g[l]), ple_w_gate[l].astype(BF16),
                   ple_w_proj[l].astype(BF16), row(final_norm_g), _tile(seq, 256), l == w_in.shape[0] - 1)
    return h.reshape(bsz, seq, d)
```

```python
import functools

import numpy as np
import jax
import jax.numpy as jnp
from jax import lax
from jax.experimental import pallas as pl
from jax.experimental.pallas import tpu as pltpu

F32 = jnp.float32
BF16 = jnp.bfloat16

RWKV_HEADS = 8
RWKV_HEAD_DIM = 64
RWKV_WIDTH = RWKV_HEADS * RWKV_HEAD_DIM
DECAY_LORA = 64
AAA_LORA = 64
GATE_LORA = 128
RWKV_COLS = 3 * RWKV_WIDTH + DECAY_LORA + AAA_LORA + GATE_LORA
GN_EPS = 64e-5
CONV_WIDTH = 512
CONV_K = 31
CONV_HALO = 32
LN_EPS = 1e-5
RMS_EPS = 1e-6
PEER_HEADS = 8
N_KEYS = 128
PEER_HALF = 64
PEER_TOPK = 16

VMEM_LIMIT = 56 * 1024 * 1024
LANES = 128


def _cparams(sem):
    return pltpu.CompilerParams(dimension_semantics=sem, vmem_limit_bytes=VMEM_LIMIT)


def _dot(a, b):
    return jnp.dot(a, b, preferred_element_type=F32)


def _dot_nt(a, b):
    return lax.dot_general(a, b, (((1,), (1,)), ((), ())), preferred_element_type=F32)


def _dot_split(x, m_bf16):
    hi = x.astype(BF16)
    lo = (x - hi.astype(F32)).astype(BF16)
    return _dot(hi, m_bf16) + _dot(lo, m_bf16)


def _rms(x, g):
    return x * lax.rsqrt(jnp.mean(x * x, axis=-1, keepdims=True) + RMS_EPS) * g


def _in_proj_kernel(x_ref, g_ref, w_ref, zr_ref, hg_ref, sg_ref):
    a = _rms(x_ref[...], g_ref[...]).astype(BF16)
    c0, c1, c2 = RWKV_COLS, RWKV_COLS + CONV_WIDTH, RWKV_COLS + 2 * CONV_WIDTH
    zr_ref[...] = _dot(a, w_ref[:, :c0])
    u = _dot(a, w_ref[:, c0:c1])
    gate = _dot(a, w_ref[:, c1:c2])
    hg_ref[...] = u * jax.nn.sigmoid(gate)
    sg_ref[...] = jax.nn.sigmoid(_dot(a, w_ref[:, c2:])).astype(BF16)


def _in_proj(x, g, w_in, tm):
    n, d = x.shape
    cols = w_in.shape[1]
    gate_cols = cols - RWKV_COLS - 2 * CONV_WIDTH
    row = lambda w: pl.BlockSpec((tm, w), lambda i: (i, 0))
    full = lambda a: pl.BlockSpec(a.shape, lambda i: (0,) * a.ndim)
    return pl.pallas_call(
        _in_proj_kernel,
        grid=(n // tm,),
        in_specs=[row(d), full(g), full(w_in)],
        out_specs=[row(RWKV_COLS), row(CONV_WIDTH), row(gate_cols)],
        out_shape=[jax.ShapeDtypeStruct((n, RWKV_COLS), F32),
                   jax.ShapeDtypeStruct((n, CONV_WIDTH), F32),
                   jax.ShapeDtypeStruct((n, gate_cols), BF16)],
        compiler_params=_cparams(("arbitrary",)),
        name="in_proj",
    )(x, g, w_in)


def _rwkv_prep_kernel(tiles_per_seq, z_ref, mu_ref, w0_ref, wup_ref, a0_ref, aup_ref, gup_ref,
                      kk_ref, ka_ref, rk_ref, seg_ref,
                      r_out, w_out, k_out, v_out, nkk_out, kka_out, g_out, bonus_out, last_ref):
    i = pl.program_id(0)

    @pl.when(i % tiles_per_seq == 0)
    def _():
        last_ref[...] = jnp.zeros_like(last_ref)

    z = z_ref[...]
    tm = z.shape[0]
    rows = lax.broadcasted_iota(jnp.int32, z.shape, 0)
    zs = jnp.where(rows == 0, last_ref[...], pltpu.roll(z, 1, axis=0))
    last_ref[...] = z[tm - 1:tm, :]
    z = z + (zs - z) * mu_ref[...]

    s0 = RWKV_WIDTH
    r = z[:, :s0]
    k = z[:, s0:2 * s0]
    v = z[:, 2 * s0:3 * s0]
    o = 3 * s0
    wl = z[:, o:o + DECAY_LORA]
    al = z[:, o + DECAY_LORA:o + DECAY_LORA + AAA_LORA]
    gl = z[:, o + DECAY_LORA + AAA_LORA:]

    hp = lax.Precision.HIGHEST
    w_log = -jax.nn.softplus(-(w0_ref[...] + jnp.dot(jnp.tanh(wl), wup_ref[...], precision=hp))) - 0.5
    decay = jnp.exp(-jnp.exp(w_log))
    a = jax.nn.sigmoid(a0_ref[...] + jnp.dot(al, aup_ref[...], precision=hp))
    g = jnp.dot(jax.nn.sigmoid(gl), gup_ref[...], precision=hp)

    seg = seg_ref[...]
    kk = k * kk_ref[...]
    kk = kk * lax.rsqrt(jnp.maximum(_dot_split(kk * kk, seg), 1e-24))
    k2 = k * (1.0 + (a - 1.0) * ka_ref[...])
    bonus = _dot_split(r * k2 * rk_ref[...], seg) * v

    r_out[...] = r
    w_out[...] = decay
    k_out[...] = k2
    v_out[...] = v
    nkk_out[...] = -kk
    kka_out[...] = kk * a
    g_out[...] = g
    bonus_out[...] = bonus


def _rwkv_prep(zr, mu, w0, w_up, a0, a_up, g_up, k_k, k_a, r_k, seg, tm, seq):
    n = zr.shape[0]
    row = lambda w: pl.BlockSpec((tm, w), lambda i: (i, 0))
    full = lambda a: pl.BlockSpec(a.shape, lambda i: (0,) * a.ndim)
    consts = (mu, w0, w_up, a0, a_up, g_up, k_k, k_a, r_k, seg)
    return pl.pallas_call(
        functools.partial(_rwkv_prep_kernel, seq // tm),
        grid=(n // tm,),
        in_specs=[row(RWKV_COLS)] + [full(c) for c in consts],
        out_specs=[row(RWKV_WIDTH)] * 8,
        out_shape=[jax.ShapeDtypeStruct((n, RWKV_WIDTH), F32)] * 8,
        scratch_shapes=[pltpu.VMEM((1, RWKV_COLS), F32)],
        compiler_params=_cparams(("arbitrary",)),
        name="rwkv_prep",
    )(zr, *consts)


def _rwkv_scan_kernel(r_ref, w_ref, k_ref, v_ref, kka_ref, nkkn_ref, y_ref, s_ref, sa_ref):
    @pl.when(pl.program_id(0) == 0)
    def _():
        s_ref[...] = jnp.zeros_like(s_ref)
        sa_ref[...] = jnp.zeros_like(sa_ref)

    steps, hd, _ = r_ref.shape

    def step(t, carry):
        sa_old = sa_ref[...]
        v_t = v_ref[t]
        y = [jnp.zeros_like(sa_old), jnp.zeros_like(sa_old)]
        sa = [jnp.zeros_like(sa_old), jnp.zeros_like(sa_old)]
        for kx in range(hd):
            row = lambda ref: ref[t, pl.ds(kx, 1), :]
            s_k = s_ref[kx] * row(w_ref) + sa_old * row(kka_ref) + v_t * row(k_ref)
            s_ref[kx] = s_k
            y[kx % 2] = y[kx % 2] + s_k * row(r_ref)
            sa[kx % 2] = sa[kx % 2] + s_k * row(nkkn_ref)
        y_ref[t] = y[0] + y[1]
        sa_ref[...] = sa[0] + sa[1]
        return carry

    lax.fori_loop(0, steps, step, 0)


def _rwkv_scan(r, w, k, v, kka, nkkn, steps):
    t, hd, chains = r.shape
    blk = pl.BlockSpec((steps, hd, chains), lambda i: (i, 0, 0))
    return pl.pallas_call(
        _rwkv_scan_kernel,
        grid=(t // steps,),
        in_specs=[blk] * 6,
        out_specs=blk,
        out_shape=jax.ShapeDtypeStruct((t, hd, chains), F32),
        scratch_shapes=[pltpu.VMEM((hd, hd, chains), F32), pltpu.VMEM((hd, chains), F32)],
        compiler_params=_cparams(("arbitrary",)),
        name="rwkv_scan",
    )(r, w, k, v, kka, nkkn)


def _mix_out_kernel(tiles_per_seq, x_ref, y_ref, g_ref, bonus_ref, hg_ref, sg_ref, seg_ref,
                    lnxg_ref, lnxb_ref, wor_ref, dww_ref, dwb_ref, clng_ref, clnb_ref, woc_ref,
                    wout_ref, nfg_ref, wq_ref, h1_out, ct_out, q_out, buf_ref):
    i = pl.program_id(0)
    tm = x_ref.shape[0]

    @pl.when(i % tiles_per_seq == 0)
    def _():
        buf_ref[:CONV_HALO, :] = jnp.zeros((CONV_HALO, CONV_WIDTH), F32)

    seg = seg_ref[...]
    y = y_ref[...]
    inv_n = 1.0 / RWKV_HEAD_DIM
    mean = _dot_split(y, seg) * inv_n
    yc = y - mean
    var = _dot_split(yc * yc, seg) * inv_n
    yn = yc * lax.rsqrt(var + GN_EPS) * lnxg_ref[...] + lnxb_ref[...]
    o_r = _dot(((yn + bonus_ref[...]) * g_ref[...]).astype(BF16), wor_ref[...])

    buf_ref[CONV_HALO:, :] = hg_ref[...]
    acc = jnp.zeros((tm, CONV_WIDTH), F32) + dwb_ref[...]
    for j in range(CONV_K):
        off = CONV_HALO - (CONV_K - 1) + j
        acc = acc + buf_ref[off:off + tm, :] * dww_ref[j:j + 1, :]
    buf_ref[:CONV_HALO, :] = buf_ref[tm:tm + CONV_HALO, :]
    mu = jnp.mean(acc, axis=-1, keepdims=True)
    ac = acc - mu
    cv = jnp.mean(ac * ac, axis=-1, keepdims=True)
    hf = ac * lax.rsqrt(cv + LN_EPS) * clng_ref[...] + clnb_ref[...]
    o_c = _dot((hf * jax.nn.sigmoid(hf)).astype(BF16), woc_ref[...])

    d = x_ref.shape[1]
    sg = sg_ref[...].astype(F32)
    mix = sg[:, :d] * o_r + sg[:, d:] * o_c
    h1 = x_ref[...] + _dot(mix.astype(BF16), wout_ref[...])
    h1_out[...] = h1
    c = _rms(h1, nfg_ref[...])
    ct_out[...] = c.T.astype(BF16)
    q_out[...] = _dot(c.astype(BF16), wq_ref[...])


def _mix_out(x, y, g, bonus, hg, sg, seg, consts, tm, seq):
    n, d = x.shape
    row = lambda w: pl.BlockSpec((tm, w), lambda i: (i, 0))
    full = lambda a: pl.BlockSpec(a.shape, lambda i: (0,) * a.ndim)
    return pl.pallas_call(
        functools.partial(_mix_out_kernel, seq // tm),
        grid=(n // tm,),
        in_specs=[row(d), row(RWKV_WIDTH), row(RWKV_WIDTH), row(RWKV_WIDTH), row(CONV_WIDTH), row(2 * d),
                  full(seg)] + [full(c) for c in consts],
        out_specs=[row(d), pl.BlockSpec((d, tm), lambda i: (0, i)), row(d)],
        out_shape=[jax.ShapeDtypeStruct((n, d), F32), jax.ShapeDtypeStruct((d, n), BF16),
                   jax.ShapeDtypeStruct((n, d), F32)],
        scratch_shapes=[pltpu.VMEM((tm + CONV_HALO, CONV_WIDTH), F32)],
        compiler_params=_cparams(("arbitrary",)),
        name="mix_out",
    )(x, y, g, bonus, hg, sg, seg, *consts)


def _sort16_pairs():
    n, pairs, p = PEER_TOPK, [], 1
    while p < n:
        k = p
        while k >= 1:
            for j in range(k % p, n - k, 2 * k):
                for i in range(min(k, n - j - k)):
                    if (i + j) // (2 * p) == (i + j + k) // (2 * p):
                        pairs.append((i + j, i + j + k))
            k //= 2
        p *= 2
    return pairs


_SORT16 = _sort16_pairs()
_BITONIC16 = [(i, i + dd) for dd in (8, 4, 2, 1) for i in range(PEER_TOPK) if not i & dd]
_CELLS = [(i, j) for i in range(PEER_TOPK) for j in range(PEER_TOPK) if (i + 1) * (j + 1) <= PEER_TOPK]


def _exchange(vals, pairs):
    for a, b in pairs:
        hi, lo = jnp.maximum(vals[a], vals[b]), jnp.minimum(vals[a], vals[b])
        vals[a], vals[b] = hi, lo
    return vals


def _top16_sorted(slabs):
    groups = []
    for s in range(0, len(slabs), PEER_TOPK):
        groups.append(_exchange(list(slabs[s:s + PEER_TOPK]), _SORT16))
    while len(groups) > 1:
        merged = []
        for a, b in zip(groups[::2], groups[1::2]):
            c = [jnp.maximum(a[i], b[PEER_TOPK - 1 - i]) for i in range(PEER_TOPK)]
            merged.append(_exchange(c, _BITONIC16))
        groups = merged
    return groups[0]


def _peer_route_kernel(q_ref, ska0_ref, ska1_ref, skb1_ref, n_out, e1_out, r2_out, e2_out, s1_ref, s2_ref, v2_ref):
    qh = q_ref[...].astype(BF16)
    half = qh.shape[1] // 2
    s1_ref[...] = _dot_nt(ska0_ref[...], qh[:, :half])
    s2_ref[...] = _dot_nt(ska1_ref[...], qh[:, half:])
    nh = PEER_HEADS
    slab = lambda ref, n: ref[n * nh:(n + 1) * nh, :]

    v1 = _top16_sorted([slab(s1_ref, n) for n in range(N_KEYS)])
    v2 = _top16_sorted([slab(s2_ref, n) for n in range(N_KEYS)])

    cand = {c: v1[c[0]] + v2[c[1]] for c in _CELLS}
    cnt = {c: jnp.full_like(v1[0], float((c[0] + 1) * (c[1] + 1) - 1)) for c in _CELLS}
    for c1 in _CELLS:
        for c2 in _CELLS:
            if c1[0] < c2[0] and c1[1] > c2[1]:
                first = cand[c1] >= cand[c2]
                cnt[c2] = cnt[c2] + jnp.where(first, 1.0, 0.0)
                cnt[c1] = cnt[c1] + jnp.where(first, 0.0, 1.0)
    top = cand[(0, 0)]
    z = jnp.zeros_like(top)
    taken = [jnp.zeros_like(top) for _ in range(PEER_TOPK)]
    for c in _CELLS:
        sel = cnt[c] < float(PEER_TOPK)
        z = z + jnp.where(sel, jnp.exp(cand[c] - top), 0.0)
        taken[c[0]] = taken[c[0]] + jnp.where(sel, 1.0, 0.0)
    inv_z = 1.0 / z

    for n in range(N_KEYS):
        s = slab(s1_ref, n)
        cnt_a = jnp.zeros_like(s)
        for i in reversed(range(PEER_TOPK)):
            cnt_a = jnp.where(s == v1[i], taken[i], cnt_a)
        n_out[n * nh:(n + 1) * nh, :] = cnt_a
        e1_out[n * nh:(n + 1) * nh, :] = jnp.exp(s - v1[0]) * inv_z

    for j in range(PEER_TOPK):
        v2_ref[j] = v2[j]
    s2b = _dot_nt(skb1_ref[...], qh[:, half:])
    for h in range(nh):
        s = s2b[h * N_KEYS:(h + 1) * N_KEYS, :]
        rank = jnp.zeros_like(s)
        for j in range(PEER_TOPK):
            rank = jnp.where(v2_ref[j, h:h + 1, :] > s, float(j + 1), rank)
        r2_out[h * N_KEYS:(h + 1) * N_KEYS, :] = rank.astype(BF16)
        e2_out[h * N_KEYS:(h + 1) * N_KEYS, :] = jnp.exp(s - v2_ref[0, h:h + 1, :]).astype(BF16)


def _peer_route(q, ska0, ska1, skb1, tn):
    n, d = q.shape
    rows = PEER_HEADS * N_KEYS
    full = lambda a: pl.BlockSpec(a.shape, lambda i: (0,) * a.ndim)
    col = pl.BlockSpec((rows, tn), lambda i: (0, i))
    return pl.pallas_call(
        _peer_route_kernel,
        grid=(n // tn,),
        in_specs=[pl.BlockSpec((tn, d), lambda i: (i, 0)), full(ska0), full(ska1), full(skb1)],
        out_specs=[col] * 4,
        out_shape=[jax.ShapeDtypeStruct((rows, n), F32), jax.ShapeDtypeStruct((rows, n), F32),
                   jax.ShapeDtypeStruct((rows, n), BF16), jax.ShapeDtypeStruct((rows, n), BF16)],
        scratch_shapes=[pltpu.VMEM((rows, tn), F32), pltpu.VMEM((rows, tn), F32),
                        pltpu.VMEM((PEER_TOPK, PEER_HEADS, tn), F32)],
        compiler_params=_cparams(("arbitrary",)),
        name="peer_route",
    )(q, ska0, ska1, skb1)


def _peer_dense_kernel(n_eb, ct_ref, u_ref, vt_ref, n_ref, e1_ref, r2_ref, e2_ref, out_ref,
                       ht0_ref, ht1_ref, cf0_ref, cf1_ref, nb_ref, eb_ref):
    s = pl.program_id(0)

    @pl.when(s == 0)
    def _():
        for ref in (ht0_ref, ht1_ref, cf0_ref, cf1_ref):
            ref[...] = jnp.zeros_like(ref)

    @pl.when((s <= 2) | ((s - 2) % n_eb == 0))
    def _():
        out_ref[...] = jnp.zeros_like(out_ref)

    nh = PEER_HEADS
    tn = ct_ref.shape[1]
    rg = 16
    a_blk = u_ref.shape[0] // N_KEYS
    zero = jnp.zeros((rg, tn), BF16)
    g_half = N_KEYS // rg // 2

    def step(ht_new, ht_old, cf_old, cf_new):
        n_chunk = a_blk // 2
        e_rows, d_rows = u_ref.shape[0] // n_chunk, out_ref.shape[0] // n_chunk
        for ch in range(n_chunk):
            rows = slice(ch * e_rows, (ch + 1) * e_rows)
            ht_new[rows, :] = _dot(u_ref[rows, :], ct_ref[...])
            for r in range(2 * ch * nh, (2 * ch + 2) * nh):
                for cols in [slice(lt * LANES, (lt + 1) * LANES) for lt in range(tn // LANES)]:
                    for src, dst in ((n_ref, nb_ref), (e1_ref, eb_ref)):
                        dst[r * rg:(r + 1) * rg, cols] = jnp.broadcast_to(src[r:r + 1, cols], (rg, LANES)).astype(BF16)
            for al, g0 in [(a, g) for a in range(2 * ch, 2 * ch + 2) for g in (0, g_half)]:
                gate = [zero] * g_half
                for h in range(nh):
                    r = al * nh + h
                    n_a, e1_a = nb_ref[r * rg:(r + 1) * rg, :], eb_ref[r * rg:(r + 1) * rg, :]
                    for g in range(g_half):
                        rows = slice(h * N_KEYS + (g0 + g) * rg, h * N_KEYS + (g0 + g + 1) * rg)
                        gate[g] = gate[g] + jnp.where(r2_ref[rows, :] < n_a, e2_ref[rows, :], zero) * e1_a
                for g in range(g_half):
                    rows = slice(al * N_KEYS + (g0 + g) * rg, al * N_KEYS + (g0 + g + 1) * rg)
                    ht = ht_old[rows, :]
                    act = 0.5 * ht * (1.0 + lax.erf(ht * (2.0 ** -0.5)))
                    cf_old[rows, :] = act.astype(BF16) * gate[g]
            rows = slice(ch * d_rows, (ch + 1) * d_rows)
            out_ref[rows, :] += _dot(vt_ref[rows, :], cf_new[...])

    @pl.when(s % 2 == 0)
    def _():
        step(ht0_ref, ht1_ref, cf1_ref, cf0_ref)

    @pl.when(s % 2 == 1)
    def _():
        step(ht1_ref, ht0_ref, cf0_ref, cf1_ref)


def _peer_dense(ct, u_bf, vt_bf, n_a, e1, r2, e2, tn, a_blk):
    d, n = ct.shape
    e_blk = a_blk * N_KEYS
    n_eb = u_bf.shape[0] // e_blk
    last = (n // tn) * n_eb - 1
    rows = PEER_HEADS * N_KEYS
    pair = lambda s, lag: jnp.clip(s - lag, 0, last)
    tok = lambda s, lag: pair(s, lag) // n_eb
    blk = lambda s, lag: pair(s, lag) % n_eb
    return pl.pallas_call(
        functools.partial(_peer_dense_kernel, n_eb),
        grid=(last + 3,),
        in_specs=[pl.BlockSpec((d, tn), lambda s: (0, tok(s, 0))),
                  pl.BlockSpec((e_blk, d), lambda s: (blk(s, 0), 0)),
                  pl.BlockSpec((d, e_blk), lambda s: (0, blk(s, 2))),
                  pl.BlockSpec((a_blk * PEER_HEADS, tn), lambda s: (blk(s, 1), tok(s, 1))),
                  pl.BlockSpec((a_blk * PEER_HEADS, tn), lambda s: (blk(s, 1), tok(s, 1))),
                  pl.BlockSpec((rows, tn), lambda s: (0, tok(s, 1))),
                  pl.BlockSpec((rows, tn), lambda s: (0, tok(s, 1)))],
        out_specs=pl.BlockSpec((d, tn), lambda s: (0, tok(s, 2))),
        out_shape=jax.ShapeDtypeStruct((d, n), F32),
        scratch_shapes=([pltpu.VMEM((e_blk, tn), F32)] * 2 + [pltpu.VMEM((e_blk, tn), BF16)] * 2
                        + [pltpu.VMEM((a_blk * PEER_HEADS * 16, tn), BF16)] * 2),
        compiler_params=_cparams(("arbitrary",)),
        name="peer_dense",
    )(ct, u_bf, vt_bf, n_a, e1, r2, e2)


def _final_kernel(last_layer, h1_ref, ffnt_ref, p_ref, png_ref, wg_ref, wp_ref, fng_ref, out_ref):
    h2 = h1_ref[...] + ffnt_ref[...].T
    gate = jax.nn.sigmoid(_dot(_rms(h2, png_ref[...]).astype(BF16), wg_ref[...]))
    h3 = h2 + gate * _dot(p_ref[...].astype(BF16), wp_ref[...])
    out_ref[...] = _rms(h3, fng_ref[...]) if last_layer else h3


def _final(h1, ffnt, p, png, wg, wp, fng, tm, last_layer):
    n, d = h1.shape
    full = lambda a: pl.BlockSpec(a.shape, lambda i: (0,) * a.ndim)
    return pl.pallas_call(
        functools.partial(_final_kernel, last_layer),
        grid=(n // tm,),
        in_specs=[pl.BlockSpec((tm, d), lambda i: (i, 0)), pl.BlockSpec((d, tm), lambda i: (0, i)),
                  pl.BlockSpec((tm, p.shape[1]), lambda i: (i, 0)), full(png), full(wg), full(wp), full(fng)],
        out_specs=pl.BlockSpec((tm, d), lambda i: (i, 0)),
        out_shape=jax.ShapeDtypeStruct((n, d), F32),
        compiler_params=_cparams(("arbitrary",)),
        name="final",
    )(h1, ffnt, p, png, wg, wp, fng)


def _tile(n, want):
    t = min(n, want)
    assert n % t == 0, (n, t)
    return t


def kernel(x, p, norm_mix_g, w_in, rwkv_mu, rwkv_w0, rwkv_w_up, rwkv_a0, rwkv_a_up, rwkv_g_up, rwkv_k_k, rwkv_k_a, rwkv_r_k, rwkv_lnx_g, rwkv_lnx_b, rwkv_w_o, conv_dw_w, conv_dw_b, conv_ln_g, conv_ln_b, conv_w_o, w_out, norm_ffn_g, peer_w_q, peer_sub_keys, peer_u, peer_v, ple_w_proj, ple_norm_g, ple_w_gate, final_norm_g):
    bsz, seq, d = x.shape
    n = bsz * seq
    nh, hd = RWKV_HEADS, RWKV_HEAD_DIM
    row = lambda a: a.reshape(1, -1)
    head_of = np.arange(RWKV_WIDTH) // hd
    seg = jnp.asarray(head_of[:, None] == head_of[None, :], BF16)
    h = x.reshape(n, d)

    for l in range(w_in.shape[0]):
        zr, hg, sg = _in_proj(h, row(norm_mix_g[l]), w_in[l].astype(BF16), _tile(seq, 512))
        r, w, k, v, nkk, kka, g, bonus = _rwkv_prep(
            zr, row(rwkv_mu[l]), row(rwkv_w0[l]), rwkv_w_up[l], row(rwkv_a0[l]), rwkv_a_up[l], rwkv_g_up[l],
            row(rwkv_k_k[l]), row(rwkv_k_a[l]), row(rwkv_r_k[l]), seg, _tile(seq, 512), seq)

        to_scan = lambda a: a.reshape(bsz, seq, nh, hd).transpose(1, 3, 0, 2).reshape(seq, hd, bsz * nh)
        nkk_s = to_scan(nkk)
        nkk_next = jnp.concatenate([nkk_s[1:], jnp.zeros_like(nkk_s[:1])], axis=0)
        y = _rwkv_scan(to_scan(r), to_scan(w), to_scan(k), to_scan(v), to_scan(kka), nkk_next, _tile(seq, 16))
        y = y.reshape(seq, hd, bsz, nh).transpose(2, 0, 3, 1).reshape(n, RWKV_WIDTH)

        consts = (row(rwkv_lnx_g[l]), row(rwkv_lnx_b[l]), rwkv_w_o[l].astype(BF16), conv_dw_w[l],
                  row(conv_dw_b[l]), row(conv_ln_g[l]), row(conv_ln_b[l]), conv_w_o[l].astype(BF16),
                  w_out[l].astype(BF16), row(norm_ffn_g[l]),
                  peer_w_q[l].reshape(d, PEER_HEADS, 2, PEER_HALF).transpose(0, 2, 1, 3).reshape(d, -1).astype(BF16))
        h1, ct, q = _mix_out(h, y, g, bonus, hg, sg, seg, consts, _tile(seq, 256), seq)

        sk = peer_sub_keys[l]
        eye = jnp.eye(PEER_HEADS, dtype=sk.dtype)
        blk = lambda half, order: jnp.einsum("hnd,hg->" + order, sk[:, half], eye).reshape(
            PEER_HEADS * N_KEYS, PEER_HEADS * PEER_HALF).astype(BF16)
        n_a, e1, r2, e2 = _peer_route(q, blk(0, "nhgd"), blk(1, "nhgd"), blk(1, "hngd"), _tile(n, 256))
        ffnt = _peer_dense(ct, peer_u[l].astype(BF16), peer_v[l].T.astype(BF16), n_a, e1, r2, e2,
                           _tile(n, 512), 8)
        h = _final(h1, ffnt, p[l].reshape(n, -1), row(ple_norm_g[l]), ple_w_gate[l].astype(BF16),
                   ple_w_proj[l].astype(BF16), row(final_norm_g), _tile(seq, 256), l == w_in.shape[0] - 1)
    return h.reshape(bsz, seq, d)
```

```python
import functools

import numpy as np
import jax
import jax.numpy as jnp
from jax import lax
from jax.experimental import pallas as pl
from jax.experimental.pallas import tpu as pltpu

F32 = jnp.float32
BF16 = jnp.bfloat16

RWKV_HEADS = 8
RWKV_HEAD_DIM = 64
RWKV_WIDTH = RWKV_HEADS * RWKV_HEAD_DIM
DECAY_LORA = 64
AAA_LORA = 64
GATE_LORA = 128
RWKV_COLS = 3 * RWKV_WIDTH + DECAY_LORA + AAA_LORA + GATE_LORA
GN_EPS = 64e-5
CONV_WIDTH = 512
CONV_K = 31
CONV_HALO = 32
LN_EPS = 1e-5
RMS_EPS = 1e-6
PEER_HEADS = 8
N_KEYS = 128
PEER_HALF = 64
PEER_TOPK = 16

VMEM_LIMIT = 56 * 1024 * 1024
LANES = 128


def _cparams(sem):
    return pltpu.CompilerParams(dimension_semantics=sem, vmem_limit_bytes=VMEM_LIMIT)


def _dot(a, b):
    return jnp.dot(a, b, preferred_element_type=F32)


def _dot_nt(a, b):
    return lax.dot_general(a, b, (((1,), (1,)), ((), ())), preferred_element_type=F32)


def _dot_split(x, m_bf16):
    hi = x.astype(BF16)
    lo = (x - hi.astype(F32)).astype(BF16)
    return _dot(hi, m_bf16) + _dot(lo, m_bf16)


def _rms(x, g):
    return x * lax.rsqrt(jnp.mean(x * x, axis=-1, keepdims=True) + RMS_EPS) * g


def _in_proj_kernel(x_ref, g_ref, w_ref, zr_ref, hg_ref, sg_ref):
    a = _rms(x_ref[...], g_ref[...]).astype(BF16)
    c0, c1, c2 = RWKV_COLS, RWKV_COLS + CONV_WIDTH, RWKV_COLS + 2 * CONV_WIDTH
    zr_ref[...] = _dot(a, w_ref[:, :c0])
    u = _dot(a, w_ref[:, c0:c1])
    gate = _dot(a, w_ref[:, c1:c2])
    hg_ref[...] = u * jax.nn.sigmoid(gate)
    sg_ref[...] = jax.nn.sigmoid(_dot(a, w_ref[:, c2:])).astype(BF16)


def _in_proj(x, g, w_in, tm):
    n, d = x.shape
    cols = w_in.shape[1]
    gate_cols = cols - RWKV_COLS - 2 * CONV_WIDTH
    row = lambda w: pl.BlockSpec((tm, w), lambda i: (i, 0))
    full = lambda a: pl.BlockSpec(a.shape, lambda i: (0,) * a.ndim)
    return pl.pallas_call(
        _in_proj_kernel,
        grid=(n // tm,),
        in_specs=[row(d), full(g), full(w_in)],
        out_specs=[row(RWKV_COLS), row(CONV_WIDTH), row(gate_cols)],
        out_shape=[jax.ShapeDtypeStruct((n, RWKV_COLS), F32),
                   jax.ShapeDtypeStruct((n, CONV_WIDTH), F32),
                   jax.ShapeDtypeStruct((n, gate_cols), BF16)],
        compiler_params=_cparams(("arbitrary",)),
        name="in_proj",
    )(x, g, w_in)


def _rwkv_prep_kernel(tiles_per_seq, z_ref, mu_ref, w0_ref, wup_ref, a0_ref, aup_ref, gup_ref,
                      kk_ref, ka_ref, rk_ref, seg_ref,
                      r_out, w_out, k_out, v_out, nkk_out, kka_out, g_out, bonus_out, last_ref):
    i = pl.program_id(0)

    @pl.when(i % tiles_per_seq == 0)
    def _():
        last_ref[...] = jnp.zeros_like(last_ref)

    z = z_ref[...]
    tm = z.shape[0]
    rows = lax.broadcasted_iota(jnp.int32, z.shape, 0)
    zs = jnp.where(rows == 0, last_ref[...], pltpu.roll(z, 1, axis=0))
    last_ref[...] = z[tm - 1:tm, :]
    z = z + (zs - z) * mu_ref[...]

    s0 = RWKV_WIDTH
    r = z[:, :s0]
    k = z[:, s0:2 * s0]
    v = z[:, 2 * s0:3 * s0]
    o = 3 * s0
    wl = z[:, o:o + DECAY_LORA]
    al = z[:, o + DECAY_LORA:o + DECAY_LORA + AAA_LORA]
    gl = z[:, o + DECAY_LORA + AAA_LORA:]

    hp = lax.Precision.HIGHEST
    w_log = -jax.nn.softplus(-(w0_ref[...] + jnp.dot(jnp.tanh(wl), wup_ref[...], precision=hp))) - 0.5
    decay = jnp.exp(-jnp.exp(w_log))
    a = jax.nn.sigmoid(a0_ref[...] + jnp.dot(al, aup_ref[...], precision=hp))
    g = jnp.dot(jax.nn.sigmoid(gl), gup_ref[...], precision=hp)

    seg = seg_ref[...]
    kk = k * kk_ref[...]
    kk = kk * lax.rsqrt(jnp.maximum(_dot_split(kk * kk, seg), 1e-24))
    k2 = k * (1.0 + (a - 1.0) * ka_ref[...])
    bonus = _dot_split(r * k2 * rk_ref[...], seg) * v

    r_out[...] = r
    w_out[...] = decay
    k_out[...] = k2
    v_out[...] = v
    nkk_out[...] = -kk
    kka_out[...] = kk * a
    g_out[...] = g
    bonus_out[...] = bonus


def _rwkv_prep(zr, mu, w0, w_up, a0, a_up, g_up, k_k, k_a, r_k, seg, tm, seq):
    n = zr.shape[0]
    row = lambda w: pl.BlockSpec((tm, w), lambda i: (i, 0))
    full = lambda a: pl.BlockSpec(a.shape, lambda i: (0,) * a.ndim)
    consts = (mu, w0, w_up, a0, a_up, g_up, k_k, k_a, r_k, seg)
    return pl.pallas_call(
        functools.partial(_rwkv_prep_kernel, seq // tm),
        grid=(n // tm,),
        in_specs=[row(RWKV_COLS)] + [full(c) for c in consts],
        out_specs=[row(RWKV_WIDTH)] * 8,
        out_shape=[jax.ShapeDtypeStruct((n, RWKV_WIDTH), F32)] * 8,
        scratch_shapes=[pltpu.VMEM((1, RWKV_COLS), F32)],
        compiler_params=_cparams(("arbitrary",)),
        name="rwkv_prep",
    )(zr, *consts)


def _rwkv_scan_kernel(r_ref, w_ref, k_ref, v_ref, kka_ref, nkk_ref, y_ref, s_ref, sa_ref, yt_ref, rows_ref, next_ref):
    @pl.when(pl.program_id(0) == 0)
    def _():
        s_ref[...] = jnp.zeros_like(s_ref)
        yt_ref[...] = jnp.zeros_like(yt_ref)

    bsz, steps, nh, hd = r_ref.shape
    chains = bsz * nh
    to_lanes = lambda ref, t: ref[:, t].reshape(chains, hd).T
    w_i, k_i, kka_i, r_i, nkk_i, v_i = range(6)

    def stage(t):
        t = jnp.minimum(t, steps - 1)
        for i, ref in ((w_i, w_ref), (k_i, k_ref), (kka_i, kka_ref), (r_i, r_ref), (v_i, v_ref)):
            next_ref[i] = to_lanes(ref, t)
        next_ref[nkk_i] = to_lanes(nkk_ref, jnp.minimum(t + 1, steps - 1))

    def emit(t):
        y_ref[:, t] = yt_ref[...].T.reshape(bsz, nh, hd)

    rows_ref[nkk_i] = to_lanes(nkk_ref, 0)
    part = [jnp.zeros((hd, chains), F32) for _ in range(4)]
    for kx in range(hd):
        part[kx % 4] = part[kx % 4] + s_ref[kx] * rows_ref[nkk_i, pl.ds(kx, 1), :]
    sa_ref[...] = (part[0] + part[1]) + (part[2] + part[3])
    stage(0)
    rows_ref[...] = next_ref[...]

    def step(t, carry):
        row = lambda i, kx: rows_ref[i, pl.ds(kx, 1), :]
        v_t, sa_old = rows_ref[v_i], sa_ref[...]
        y = [jnp.zeros_like(v_t), jnp.zeros_like(v_t)]
        sa = [jnp.zeros_like(v_t), jnp.zeros_like(v_t)]
        for kx in range(hd):
            s_k = s_ref[kx] * row(w_i, kx) + sa_old * row(kka_i, kx) + v_t * row(k_i, kx)
            s_ref[kx] = s_k
            y[kx % 2] = y[kx % 2] + s_k * row(r_i, kx)
            sa[kx % 2] = sa[kx % 2] + s_k * row(nkk_i, kx)
        sa_ref[...] = sa[0] + sa[1]
        emit(jnp.maximum(t - 1, 0))
        yt_ref[...] = y[0] + y[1]
        stage(t + 1)
        rows_ref[...] = next_ref[...]
        return carry

    lax.fori_loop(0, steps, step, 0)
    emit(steps - 1)


def _rwkv_scan(r, w, k, v, kka, nkk, steps):
    bsz, t, nh, hd = r.shape
    blk = pl.BlockSpec((bsz, steps, nh, hd), lambda i: (0, i, 0, 0))
    return pl.pallas_call(
        _rwkv_scan_kernel,
        grid=(t // steps,),
        in_specs=[blk] * 6,
        out_specs=blk,
        out_shape=jax.ShapeDtypeStruct((bsz, t, nh, hd), F32),
        scratch_shapes=[pltpu.VMEM((hd, hd, bsz * nh), F32), pltpu.VMEM((hd, bsz * nh), F32),
                        pltpu.VMEM((hd, bsz * nh), F32), pltpu.VMEM((6, hd, bsz * nh), F32),
                        pltpu.VMEM((6, hd, bsz * nh), F32)],
        compiler_params=_cparams(("arbitrary",)),
        name="rwkv_scan",
    )(r, w, k, v, kka, nkk)


def _mix_out_kernel(tiles_per_seq, x_ref, y_ref, g_ref, bonus_ref, hg_ref, sg_ref, seg_ref,
                    lnxg_ref, lnxb_ref, wor_ref, dww_ref, dwb_ref, clng_ref, clnb_ref, woc_ref,
                    wout_ref, nfg_ref, wq_ref, h1_out, ct_out, q_out, buf_ref):
    i = pl.program_id(0)
    tm = x_ref.shape[0]

    @pl.when(i % tiles_per_seq == 0)
    def _():
        buf_ref[:CONV_HALO, :] = jnp.zeros((CONV_HALO, CONV_WIDTH), F32)

    seg = seg_ref[...]
    y = y_ref[...]
    inv_n = 1.0 / RWKV_HEAD_DIM
    mean = _dot_split(y, seg) * inv_n
    yc = y - mean
    var = _dot_split(yc * yc, seg) * inv_n
    yn = yc * lax.rsqrt(var + GN_EPS) * lnxg_ref[...] + lnxb_ref[...]
    o_r = _dot(((yn + bonus_ref[...]) * g_ref[...]).astype(BF16), wor_ref[...])

    buf_ref[CONV_HALO:, :] = hg_ref[...]
    acc = jnp.zeros((tm, CONV_WIDTH), F32) + dwb_ref[...]
    for j in range(CONV_K):
        off = CONV_HALO - (CONV_K - 1) + j
        acc = acc + buf_ref[off:off + tm, :] * dww_ref[j:j + 1, :]
    buf_ref[:CONV_HALO, :] = buf_ref[tm:tm + CONV_HALO, :]
    mu = jnp.mean(acc, axis=-1, keepdims=True)
    ac = acc - mu
    cv = jnp.mean(ac * ac, axis=-1, keepdims=True)
    hf = ac * lax.rsqrt(cv + LN_EPS) * clng_ref[...] + clnb_ref[...]
    o_c = _dot((hf * jax.nn.sigmoid(hf)).astype(BF16), woc_ref[...])

    d = x_ref.shape[1]
    sg = sg_ref[...].astype(F32)
    mix = sg[:, :d] * o_r + sg[:, d:] * o_c
    h1 = x_ref[...] + _dot(mix.astype(BF16), wout_ref[...])
    h1_out[...] = h1
    c = _rms(h1, nfg_ref[...])
    ct_out[...] = c.T.astype(BF16)
    q_out[...] = _dot(c.astype(BF16), wq_ref[...])


def _mix_out(x, y, g, bonus, hg, sg, seg, consts, tm, seq):
    n, d = x.shape
    row = lambda w: pl.BlockSpec((tm, w), lambda i: (i, 0))
    full = lambda a: pl.BlockSpec(a.shape, lambda i: (0,) * a.ndim)
    return pl.pallas_call(
        functools.partial(_mix_out_kernel, seq // tm),
        grid=(n // tm,),
        in_specs=[row(d), row(RWKV_WIDTH), row(RWKV_WIDTH), row(RWKV_WIDTH), row(CONV_WIDTH), row(2 * d),
                  full(seg)] + [full(c) for c in consts],
        out_specs=[row(d), pl.BlockSpec((d, tm), lambda i: (0, i)), row(d)],
        out_shape=[jax.ShapeDtypeStruct((n, d), F32), jax.ShapeDtypeStruct((d, n), BF16),
                   jax.ShapeDtypeStruct((n, d), F32)],
        scratch_shapes=[pltpu.VMEM((tm + CONV_HALO, CONV_WIDTH), F32)],
        compiler_params=_cparams(("arbitrary",)),
        name="mix_out",
    )(x, y, g, bonus, hg, sg, seg, *consts)


def _sort16_pairs():
    n, pairs, p = PEER_TOPK, [], 1
    while p < n:
        k = p
        while k >= 1:
            for j in range(k % p, n - k, 2 * k):
                for i in range(min(k, n - j - k)):
                    if (i + j) // (2 * p) == (i + j + k) // (2 * p):
                        pairs.append((i + j, i + j + k))
            k //= 2
        p *= 2
    return pairs


_SORT16 = _sort16_pairs()
_BITONIC16 = [(i, i + dd) for dd in (8, 4, 2, 1) for i in range(PEER_TOPK) if not i & dd]
_CELLS = [(i, j) for i in range(PEER_TOPK) for j in range(PEER_TOPK) if (i + 1) * (j + 1) <= PEER_TOPK]


def _exchange(vals, pairs):
    for a, b in pairs:
        hi, lo = jnp.maximum(vals[a], vals[b]), jnp.minimum(vals[a], vals[b])
        vals[a], vals[b] = hi, lo
    return vals


def _top16_sorted(slabs):
    groups = []
    for s in range(0, len(slabs), PEER_TOPK):
        groups.append(_exchange(list(slabs[s:s + PEER_TOPK]), _SORT16))
    while len(groups) > 1:
        merged = []
        for a, b in zip(groups[::2], groups[1::2]):
            c = [jnp.maximum(a[i], b[PEER_TOPK - 1 - i]) for i in range(PEER_TOPK)]
            merged.append(_exchange(c, _BITONIC16))
        groups = merged
    return groups[0]


def _peer_route_kernel(q_ref, ska0_ref, ska1_ref, skb1_ref, n_out, e1_out, r2_out, e2_out, s1_ref, s2_ref, v2_ref):
    qh = q_ref[...].astype(BF16)
    half = qh.shape[1] // 2
    s1_ref[...] = _dot_nt(ska0_ref[...], qh[:, :half])
    s2_ref[...] = _dot_nt(ska1_ref[...], qh[:, half:])
    nh = PEER_HEADS
    slab = lambda ref, n: ref[n * nh:(n + 1) * nh, :]

    v1 = _top16_sorted([slab(s1_ref, n) for n in range(N_KEYS)])
    v2 = _top16_sorted([slab(s2_ref, n) for n in range(N_KEYS)])

    cand = {c: v1[c[0]] + v2[c[1]] for c in _CELLS}
    cnt = {c: jnp.full_like(v1[0], float((c[0] + 1) * (c[1] + 1) - 1)) for c in _CELLS}
    for c1 in _CELLS:
        for c2 in _CELLS:
            if c1[0] < c2[0] and c1[1] > c2[1]:
                first = cand[c1] >= cand[c2]
                cnt[c2] = cnt[c2] + jnp.where(first, 1.0, 0.0)
                cnt[c1] = cnt[c1] + jnp.where(first, 0.0, 1.0)
    top = cand[(0, 0)]
    z = jnp.zeros_like(top)
    taken = [jnp.zeros_like(top) for _ in range(PEER_TOPK)]
    for c in _CELLS:
        sel = cnt[c] < float(PEER_TOPK)
        z = z + jnp.where(sel, jnp.exp(cand[c] - top), 0.0)
        taken[c[0]] = taken[c[0]] + jnp.where(sel, 1.0, 0.0)
    inv_z = 1.0 / z

    for n in range(N_KEYS):
        s = slab(s1_ref, n)
        cnt_a = jnp.zeros_like(s)
        for i in reversed(range(PEER_TOPK)):
            cnt_a = jnp.where(s == v1[i], taken[i], cnt_a)
        n_out[n * nh:(n + 1) * nh, :] = cnt_a
        e1_out[n * nh:(n + 1) * nh, :] = jnp.exp(s - v1[0]) * inv_z

    for j in range(PEER_TOPK):
        v2_ref[j] = v2[j]
    s2b = _dot_nt(skb1_ref[...], qh[:, half:])
    for h in range(nh):
        s = s2b[h * N_KEYS:(h + 1) * N_KEYS, :]
        rank = jnp.zeros_like(s)
        for j in range(PEER_TOPK):
            rank = jnp.where(v2_ref[j, h:h + 1, :] > s, float(j + 1), rank)
        r2_out[h * N_KEYS:(h + 1) * N_KEYS, :] = rank.astype(BF16)
        e2_out[h * N_KEYS:(h + 1) * N_KEYS, :] = jnp.exp(s - v2_ref[0, h:h + 1, :]).astype(BF16)


def _peer_route(q, ska0, ska1, skb1, tn):
    n, d = q.shape
    rows = PEER_HEADS * N_KEYS
    full = lambda a: pl.BlockSpec(a.shape, lambda i: (0,) * a.ndim)
    col = pl.BlockSpec((rows, tn), lambda i: (0, i))
    return pl.pallas_call(
        _peer_route_kernel,
        grid=(n // tn,),
        in_specs=[pl.BlockSpec((tn, d), lambda i: (i, 0)), full(ska0), full(ska1), full(skb1)],
        out_specs=[col] * 4,
        out_shape=[jax.ShapeDtypeStruct((rows, n), F32), jax.ShapeDtypeStruct((rows, n), F32),
                   jax.ShapeDtypeStruct((rows, n), BF16), jax.ShapeDtypeStruct((rows, n), BF16)],
        scratch_shapes=[pltpu.VMEM((rows, tn), F32), pltpu.VMEM((rows, tn), F32),
                        pltpu.VMEM((PEER_TOPK, PEER_HEADS, tn), F32)],
        compiler_params=_cparams(("arbitrary",)),
        name="peer_route",
    )(q, ska0, ska1, skb1)


def _peer_dense_kernel(n_eb, ct_ref, u_ref, vt_ref, n_ref, e1_ref, r2_ref, e2_ref, out_ref,
                       ht0_ref, ht1_ref, cf0_ref, cf1_ref, nb_ref, eb_ref):
    s = pl.program_id(0)

    @pl.when(s == 0)
    def _():
        for ref in (ht0_ref, ht1_ref, cf0_ref, cf1_ref):
            ref[...] = jnp.zeros_like(ref)

    @pl.when((s <= 2) | ((s - 2) % n_eb == 0))
    def _():
        out_ref[...] = jnp.zeros_like(out_ref)

    nh = PEER_HEADS
    tn = ct_ref.shape[1]
    rg = 16
    a_blk = u_ref.shape[0] // N_KEYS
    zero = jnp.zeros((rg, tn), BF16)
    g_half = N_KEYS // rg // 2

    def step(ht_new, ht_old, cf_old, cf_new):
        n_chunk = a_blk // 2
        e_rows, d_rows = u_ref.shape[0] // n_chunk, out_ref.shape[0] // n_chunk
        for ch in range(n_chunk):
            rows = slice(ch * e_rows, (ch + 1) * e_rows)
            ht_new[rows, :] = _dot(u_ref[rows, :], ct_ref[...])
            for r in range(2 * ch * nh, (2 * ch + 2) * nh):
                for cols in [slice(lt * LANES, (lt + 1) * LANES) for lt in range(tn // LANES)]:
                    for src, dst in ((n_ref, nb_ref), (e1_ref, eb_ref)):
                        dst[r * rg:(r + 1) * rg, cols] = jnp.broadcast_to(src[r:r + 1, cols], (rg, LANES)).astype(BF16)
            for al, g0 in [(a, g) for a in range(2 * ch, 2 * ch + 2) for g in (0, g_half)]:
                gate = [zero] * g_half
                for h in range(nh):
                    r = al * nh + h
                    n_a, e1_a = nb_ref[r * rg:(r + 1) * rg, :], eb_ref[r * rg:(r + 1) * rg, :]
                    for g in range(g_half):
                        rows = slice(h * N_KEYS + (g0 + g) * rg, h * N_KEYS + (g0 + g + 1) * rg)
                        gate[g] = gate[g] + jnp.where(r2_ref[rows, :] < n_a, e2_ref[rows, :], zero) * e1_a
                for g in range(g_half):
                    rows = slice(al * N_KEYS + (g0 + g) * rg, al * N_KEYS + (g0 + g + 1) * rg)
                    ht = ht_old[rows, :]
                    act = 0.5 * ht * (1.0 + lax.erf(ht * (2.0 ** -0.5)))
                    cf_old[rows, :] = act.astype(BF16) * gate[g]
            rows = slice(ch * d_rows, (ch + 1) * d_rows)
            out_ref[rows, :] += _dot(vt_ref[rows, :], cf_new[...])

    @pl.when(s % 2 == 0)
    def _():
        step(ht0_ref, ht1_ref, cf1_ref, cf0_ref)

    @pl.when(s % 2 == 1)
    def _():
        step(ht1_ref, ht0_ref, cf0_ref, cf1_ref)


def _peer_dense(ct, u_bf, vt_bf, n_a, e1, r2, e2, tn, a_blk):
    d, n = ct.shape
    e_blk = a_blk * N_KEYS
    n_eb = u_bf.shape[0] // e_blk
    last = (n // tn) * n_eb - 1
    rows = PEER_HEADS * N_KEYS
    pair = lambda s, lag: jnp.clip(s - lag, 0, last)
    tok = lambda s, lag: pair(s, lag) // n_eb
    blk = lambda s, lag: pair(s, lag) % n_eb
    return pl.pallas_call(
        functools.partial(_peer_dense_kernel, n_eb),
        grid=(last + 3,),
        in_specs=[pl.BlockSpec((d, tn), lambda s: (0, tok(s, 0))),
                  pl.BlockSpec((e_blk, d), lambda s: (blk(s, 0), 0)),
                  pl.BlockSpec((d, e_blk), lambda s: (0, blk(s, 2))),
                  pl.BlockSpec((a_blk * PEER_HEADS, tn), lambda s: (blk(s, 1), tok(s, 1))),
                  pl.BlockSpec((a_blk * PEER_HEADS, tn), lambda s: (blk(s, 1), tok(s, 1))),
                  pl.BlockSpec((rows, tn), lambda s: (0, tok(s, 1))),
                  pl.BlockSpec((rows, tn), lambda s: (0, tok(s, 1)))],
        out_specs=pl.BlockSpec((d, tn), lambda s: (0, tok(s, 2))),
        out_shape=jax.ShapeDtypeStruct((d, n), F32),
        scratch_shapes=([pltpu.VMEM((e_blk, tn), F32)] * 2 + [pltpu.VMEM((e_blk, tn), BF16)] * 2
                        + [pltpu.VMEM((a_blk * PEER_HEADS * 16, tn), BF16)] * 2),
        compiler_params=_cparams(("arbitrary",)),
        name="peer_dense",
    )(ct, u_bf, vt_bf, n_a, e1, r2, e2)


def _final_kernel(last_layer, h1_ref, ffnt_ref, p_ref, png_ref, wg_ref, wp_ref, fng_ref, out_ref):
    h2 = h1_ref[...] + ffnt_ref[...].T
    gate = jax.nn.sigmoid(_dot(_rms(h2, png_ref[...]).astype(BF16), wg_ref[...]))
    h3 = h2 + gate * _dot(p_ref[...].astype(BF16), wp_ref[...])
    out_ref[...] = _rms(h3, fng_ref[...]) if last_layer else h3


def _final(h1, ffnt, p, png, wg, wp, fng, tm, last_layer):
    n, d = h1.shape
    full = lambda a: pl.BlockSpec(a.shape, lambda i: (0,) * a.ndim)
    return pl.pallas_call(
        functools.partial(_final_kernel, last_layer),
        grid=(n // tm,),
        in_specs=[pl.BlockSpec((tm, d), lambda i: (i, 0)), pl.BlockSpec((d, tm), lambda i: (0, i)),
                  pl.BlockSpec((tm, p.shape[1]), lambda i: (i, 0)), full(png), full(wg), full(wp), full(fng)],
        out_specs=pl.BlockSpec((tm, d), lambda i: (i, 0)),
        out_shape=jax.ShapeDtypeStruct((n, d), F32),
        compiler_params=_cparams(("arbitrary",)),
        name="final",
    )(h1, ffnt, p, png, wg, wp, fng)


def _tile(n, want):
    t = min(n, want)
    assert n % t == 0, (n, t)
    return t


def kernel(x, p, norm_mix_g, w_in, rwkv_mu, rwkv_w0, rwkv_w_up, rwkv_a0, rwkv_a_up, rwkv_g_up, rwkv_k_k, rwkv_k_a, rwkv_r_k, rwkv_lnx_g, rwkv_lnx_b, rwkv_w_o, conv_dw_w, conv_dw_b, conv_ln_g, conv_ln_b, conv_w_o, w_out, norm_ffn_g, peer_w_q, peer_sub_keys, peer_u, peer_v, ple_w_proj, ple_norm_g, ple_w_gate, final_norm_g):
    bsz, seq, d = x.shape
    n = bsz * seq
    nh, hd = RWKV_HEADS, RWKV_HEAD_DIM
    row = lambda a: a.reshape(1, -1)
    head_of = np.arange(RWKV_WIDTH) // hd
    seg = jnp.asarray(head_of[:, None] == head_of[None, :], BF16)
    h = x.reshape(n, d)

    for l in range(w_in.shape[0]):
        zr, hg, sg = _in_proj(h, row(norm_mix_g[l]), w_in[l].astype(BF16), _tile(seq, 512))
        r, w, k, v, nkk, kka, g, bonus = _rwkv_prep(
            zr, row(rwkv_mu[l]), row(rwkv_w0[l]), rwkv_w_up[l], row(rwkv_a0[l]), rwkv_a_up[l], rwkv_g_up[l],
            row(rwkv_k_k[l]), row(rwkv_k_a[l]), row(rwkv_r_k[l]), seg, _tile(seq, 512), seq)

        heads = lambda a: a.reshape(bsz, seq, nh, hd)
        y = _rwkv_scan(heads(r), heads(w), heads(k), heads(v), heads(kka), heads(nkk), _tile(seq, 16))
        y = y.reshape(n, RWKV_WIDTH)

        consts = (row(rwkv_lnx_g[l]), row(rwkv_lnx_b[l]), rwkv_w_o[l].astype(BF16), conv_dw_w[l],
                  row(conv_dw_b[l]), row(conv_ln_g[l]), row(conv_ln_b[l]), conv_w_o[l].astype(BF16),
                  w_out[l].astype(BF16), row(norm_ffn_g[l]),
                  peer_w_q[l].reshape(d, PEER_HEADS, 2, PEER_HALF).transpose(0, 2, 1, 3).reshape(d, -1).astype(BF16))
        h1, ct, q = _mix_out(h, y, g, bonus, hg, sg, seg, consts, _tile(seq, 256), seq)

        sk = peer_sub_keys[l]
        eye = jnp.eye(PEER_HEADS, dtype=sk.dtype)
        blk = lambda half, order: jnp.einsum("hnd,hg->" + order, sk[:, half], eye).reshape(
            PEER_HEADS * N_KEYS, PEER_HEADS * PEER_HALF).astype(BF16)
        n_a, e1, r2, e2 = _peer_route(q, blk(0, "nhgd"), blk(1, "nhgd"), blk(1, "hngd"), _tile(n, 256))
        ffnt = _peer_dense(ct, peer_u[l].astype(BF16), peer_v[l].T.astype(BF16), n_a, e1, r2, e2,
                           _tile(n, 512), 8)
        h = _final(h1, ffnt, p[l].reshape(n, -1), row(ple_norm_g[l]), ple_w_gate[l].astype(BF16),
                   ple_w_proj[l].astype(BF16), row(final_norm_g), _tile(seq, 256), l == w_in.shape[0] - 1)
    return h.reshape(bsz, seq, d)
```

```python
import functools

import numpy as np
import jax
import jax.numpy as jnp
from jax import lax
from jax.experimental import pallas as pl
from jax.experimental.pallas import tpu as pltpu

F32 = jnp.float32
BF16 = jnp.bfloat16

RWKV_HEADS = 8
RWKV_HEAD_DIM = 64
RWKV_WIDTH = RWKV_HEADS * RWKV_HEAD_DIM
DECAY_LORA = 64
AAA_LORA = 64
GATE_LORA = 128
RWKV_COLS = 3 * RWKV_WIDTH + DECAY_LORA + AAA_LORA + GATE_LORA
GN_EPS = 64e-5
CONV_WIDTH = 512
CONV_K = 31
CONV_HALO = 32
LN_EPS = 1e-5
RMS_EPS = 1e-6
PEER_HEADS = 8
N_KEYS = 128
PEER_HALF = 64
PEER_TOPK = 16

VMEM_LIMIT = 56 * 1024 * 1024
LANES = 128
SUBLANES = 8


def _cparams(sem):
    return pltpu.CompilerParams(dimension_semantics=sem, vmem_limit_bytes=VMEM_LIMIT)


def _dot(a, b):
    return jnp.dot(a, b, preferred_element_type=F32)


def _dot_nt(a, b):
    return lax.dot_general(a, b, (((1,), (1,)), ((), ())), preferred_element_type=F32)


def _dot_split(x, m_bf16):
    hi = x.astype(BF16)
    lo = (x - hi.astype(F32)).astype(BF16)
    return _dot(hi, m_bf16) + _dot(lo, m_bf16)


def _rms(x, g):
    return x * lax.rsqrt(jnp.mean(x * x, axis=-1, keepdims=True) + RMS_EPS) * g


def _in_proj_kernel(x_ref, g_ref, w_ref, zr_ref, hg_ref, sg_ref):
    a = _rms(x_ref[...], g_ref[...]).astype(BF16)
    c0, c1, c2 = RWKV_COLS, RWKV_COLS + CONV_WIDTH, RWKV_COLS + 2 * CONV_WIDTH
    zr_ref[...] = _dot(a, w_ref[:, :c0])
    u = _dot(a, w_ref[:, c0:c1])
    gate = _dot(a, w_ref[:, c1:c2])
    hg_ref[...] = u * jax.nn.sigmoid(gate)
    sg_ref[...] = jax.nn.sigmoid(_dot(a, w_ref[:, c2:])).astype(BF16)


def _in_proj(x, g, w_in, tm):
    n, d = x.shape
    cols = w_in.shape[1]
    gate_cols = cols - RWKV_COLS - 2 * CONV_WIDTH
    row = lambda w: pl.BlockSpec((tm, w), lambda i: (i, 0))
    full = lambda a: pl.BlockSpec(a.shape, lambda i: (0,) * a.ndim)
    return pl.pallas_call(
        _in_proj_kernel,
        grid=(n // tm,),
        in_specs=[row(d), full(g), full(w_in)],
        out_specs=[row(RWKV_COLS), row(CONV_WIDTH), row(gate_cols)],
        out_shape=[jax.ShapeDtypeStruct((n, RWKV_COLS), F32),
                   jax.ShapeDtypeStruct((n, CONV_WIDTH), F32),
                   jax.ShapeDtypeStruct((n, gate_cols), BF16)],
        compiler_params=_cparams(("arbitrary",)),
        name="in_proj",
    )(x, g, w_in)


def _rwkv_prep_kernel(tiles_per_seq, z_ref, mu_ref, w0_ref, wup_ref, a0_ref, aup_ref, gup_ref,
                      kk_ref, ka_ref, rk_ref, seg_ref,
                      r_out, w_out, k_out, v_out, nkk_out, kka_out, g_out, bonus_out, last_ref):
    i = pl.program_id(0)

    @pl.when(i % tiles_per_seq == 0)
    def _():
        last_ref[...] = jnp.zeros_like(last_ref)

    z = z_ref[...]
    tm = z.shape[0]
    rows = lax.broadcasted_iota(jnp.int32, z.shape, 0)
    zs = jnp.where(rows == 0, last_ref[...], pltpu.roll(z, 1, axis=0))
    last_ref[...] = z[tm - 1:tm, :]
    z = z + (zs - z) * mu_ref[...]

    s0 = RWKV_WIDTH
    r = z[:, :s0]
    k = z[:, s0:2 * s0]
    v = z[:, 2 * s0:3 * s0]
    o = 3 * s0
    wl = z[:, o:o + DECAY_LORA]
    al = z[:, o + DECAY_LORA:o + DECAY_LORA + AAA_LORA]
    gl = z[:, o + DECAY_LORA + AAA_LORA:]

    hp = lax.Precision.HIGHEST
    w_log = -jax.nn.softplus(-(w0_ref[...] + jnp.dot(jnp.tanh(wl), wup_ref[...], precision=hp))) - 0.5
    decay = jnp.exp(-jnp.exp(w_log))
    a = jax.nn.sigmoid(a0_ref[...] + jnp.dot(al, aup_ref[...], precision=hp))
    g = jnp.dot(jax.nn.sigmoid(gl), gup_ref[...], precision=hp)

    seg = seg_ref[...]
    kk = k * kk_ref[...]
    kk = kk * lax.rsqrt(jnp.maximum(_dot_split(kk * kk, seg), 1e-24))
    k2 = k * (1.0 + (a - 1.0) * ka_ref[...])
    bonus = _dot_split(r * k2 * rk_ref[...], seg) * v

    r_out[...] = r
    w_out[...] = decay
    k_out[...] = k2
    v_out[...] = v
    nkk_out[...] = -kk
    kka_out[...] = kk * a
    g_out[...] = g
    bonus_out[...] = bonus


def _rwkv_prep(zr, mu, w0, w_up, a0, a_up, g_up, k_k, k_a, r_k, seg, tm, seq):
    n = zr.shape[0]
    row = lambda w: pl.BlockSpec((tm, w), lambda i: (i, 0))
    full = lambda a: pl.BlockSpec(a.shape, lambda i: (0,) * a.ndim)
    consts = (mu, w0, w_up, a0, a_up, g_up, k_k, k_a, r_k, seg)
    return pl.pallas_call(
        functools.partial(_rwkv_prep_kernel, seq // tm),
        grid=(n // tm,),
        in_specs=[row(RWKV_COLS)] + [full(c) for c in consts],
        out_specs=[row(RWKV_WIDTH)] * 8,
        out_shape=[jax.ShapeDtypeStruct((n, RWKV_WIDTH), F32)] * 8,
        scratch_shapes=[pltpu.VMEM((1, RWKV_COLS), F32)],
        compiler_params=_cparams(("arbitrary",)),
        name="rwkv_prep",
    )(zr, *consts)


def _rwkv_scan_kernel(r_ref, w_ref, k_ref, v_ref, kka_ref, nkk_ref, y_ref, s_ref, sa_ref, yt_ref, rows_ref, next_ref):
    @pl.when(pl.program_id(0) == 0)
    def _():
        s_ref[...] = jnp.zeros_like(s_ref)
        yt_ref[...] = jnp.zeros_like(yt_ref)

    bsz, steps, nh, hd = r_ref.shape
    chains = bsz * nh
    to_lanes = lambda ref, t: ref[:, t].reshape(chains, hd).T
    w_i, k_i, kka_i, r_i, nkk_i, v_i = range(6)

    def stage(t):
        t = jnp.minimum(t, steps - 1)
        for i, ref in ((w_i, w_ref), (k_i, k_ref), (kka_i, kka_ref), (r_i, r_ref), (v_i, v_ref)):
            next_ref[i] = to_lanes(ref, t)
        next_ref[nkk_i] = to_lanes(nkk_ref, jnp.minimum(t + 1, steps - 1))

    def emit(t):
        y_ref[:, t] = yt_ref[...].T.reshape(bsz, nh, hd)

    rows_ref[nkk_i] = to_lanes(nkk_ref, 0)
    part = [jnp.zeros((hd, chains), F32) for _ in range(4)]
    for kx in range(hd):
        part[kx % 4] = part[kx % 4] + s_ref[kx] * rows_ref[nkk_i, pl.ds(kx, 1), :]
    sa_ref[...] = (part[0] + part[1]) + (part[2] + part[3])
    stage(0)
    rows_ref[...] = next_ref[...]

    def step(t, carry):
        row = lambda i, kx: rows_ref[i, pl.ds(kx, 1), :]
        v_t, sa_old = rows_ref[v_i], sa_ref[...]
        y = [jnp.zeros_like(v_t), jnp.zeros_like(v_t)]
        sa = [jnp.zeros_like(v_t), jnp.zeros_like(v_t)]
        for kx in range(hd):
            s_k = s_ref[kx] * row(w_i, kx) + sa_old * row(kka_i, kx) + v_t * row(k_i, kx)
            s_ref[kx] = s_k
            y[kx % 2] = y[kx % 2] + s_k * row(r_i, kx)
            sa[kx % 2] = sa[kx % 2] + s_k * row(nkk_i, kx)
        sa_ref[...] = sa[0] + sa[1]
        emit(jnp.maximum(t - 1, 0))
        yt_ref[...] = y[0] + y[1]
        stage(t + 1)
        rows_ref[...] = next_ref[...]
        return carry

    lax.fori_loop(0, steps, step, 0)
    emit(steps - 1)


def _rwkv_scan(r, w, k, v, kka, nkk, steps):
    bsz, t, nh, hd = r.shape
    blk = pl.BlockSpec((bsz, steps, nh, hd), lambda i: (0, i, 0, 0))
    return pl.pallas_call(
        _rwkv_scan_kernel,
        grid=(t // steps,),
        in_specs=[blk] * 6,
        out_specs=blk,
        out_shape=jax.ShapeDtypeStruct((bsz, t, nh, hd), F32),
        scratch_shapes=[pltpu.VMEM((hd, hd, bsz * nh), F32), pltpu.VMEM((hd, bsz * nh), F32),
                        pltpu.VMEM((hd, bsz * nh), F32), pltpu.VMEM((6, hd, bsz * nh), F32),
                        pltpu.VMEM((6, hd, bsz * nh), F32)],
        compiler_params=_cparams(("arbitrary",)),
        name="rwkv_scan",
    )(r, w, k, v, kka, nkk)


def _mix_out_kernel(tiles_per_seq, x_ref, y_ref, g_ref, bonus_ref, hg_ref, sg_ref, seg_ref,
                    lnxg_ref, lnxb_ref, wor_ref, dww_ref, dwb_ref, clng_ref, clnb_ref, woc_ref,
                    wout_ref, nfg_ref, wq_ref, h1_out, ct_out, q_out, buf_ref):
    i = pl.program_id(0)
    tm = x_ref.shape[0]

    @pl.when(i % tiles_per_seq == 0)
    def _():
        buf_ref[:CONV_HALO, :] = jnp.zeros((CONV_HALO, CONV_WIDTH), F32)
        buf_ref[CONV_HALO + tm:, :] = jnp.zeros((SUBLANES, CONV_WIDTH), F32)

    seg = seg_ref[...]
    y = y_ref[...]
    inv_n = 1.0 / RWKV_HEAD_DIM
    mean = _dot_split(y, seg) * inv_n
    yc = y - mean
    var = _dot_split(yc * yc, seg) * inv_n
    yn = yc * lax.rsqrt(var + GN_EPS) * lnxg_ref[...] + lnxb_ref[...]
    o_r = _dot(((yn + bonus_ref[...]) * g_ref[...]).astype(BF16), wor_ref[...])

    buf_ref[CONV_HALO:CONV_HALO + tm, :] = hg_ref[...]
    acc = jnp.zeros((tm, CONV_WIDTH), F32) + dwb_ref[...]
    for phase in range(SUBLANES):
        part = None
        for j in range(CONV_K):
            off = CONV_HALO - (CONV_K - 1) + j
            if off % SUBLANES == phase:
                base = off - phase
                term = buf_ref[base:base + tm + SUBLANES, :] * dww_ref[j:j + 1, :]
                part = term if part is None else part + term
        acc = acc + part[phase:phase + tm, :]
    buf_ref[:CONV_HALO, :] = buf_ref[tm:tm + CONV_HALO, :]
    mu = jnp.mean(acc, axis=-1, keepdims=True)
    ac = acc - mu
    cv = jnp.mean(ac * ac, axis=-1, keepdims=True)
    hf = ac * lax.rsqrt(cv + LN_EPS) * clng_ref[...] + clnb_ref[...]
    o_c = _dot((hf * jax.nn.sigmoid(hf)).astype(BF16), woc_ref[...])

    d = x_ref.shape[1]
    sg = sg_ref[...].astype(F32)
    mix = sg[:, :d] * o_r + sg[:, d:] * o_c
    h1 = x_ref[...] + _dot(mix.astype(BF16), wout_ref[...])
    h1_out[...] = h1
    c = _rms(h1, nfg_ref[...])
    ct_out[...] = c.T.astype(BF16)
    q_out[...] = _dot(c.astype(BF16), wq_ref[...])


def _mix_out(x, y, g, bonus, hg, sg, seg, consts, tm, seq):
    n, d = x.shape
    row = lambda w: pl.BlockSpec((tm, w), lambda i: (i, 0))
    full = lambda a: pl.BlockSpec(a.shape, lambda i: (0,) * a.ndim)
    return pl.pallas_call(
        functools.partial(_mix_out_kernel, seq // tm),
        grid=(n // tm,),
        in_specs=[row(d), row(RWKV_WIDTH), row(RWKV_WIDTH), row(RWKV_WIDTH), row(CONV_WIDTH), row(2 * d),
                  full(seg)] + [full(c) for c in consts],
        out_specs=[row(d), pl.BlockSpec((d, tm), lambda i: (0, i)), row(d)],
        out_shape=[jax.ShapeDtypeStruct((n, d), F32), jax.ShapeDtypeStruct((d, n), BF16),
                   jax.ShapeDtypeStruct((n, d), F32)],
        scratch_shapes=[pltpu.VMEM((tm + CONV_HALO + SUBLANES, CONV_WIDTH), F32)],
        compiler_params=_cparams(("arbitrary",)),
        name="mix_out",
    )(x, y, g, bonus, hg, sg, seg, *consts)


def _sort16_pairs():
    n, pairs, p = PEER_TOPK, [], 1
    while p < n:
        k = p
        while k >= 1:
            for j in range(k % p, n - k, 2 * k):
                for i in range(min(k, n - j - k)):
                    if (i + j) // (2 * p) == (i + j + k) // (2 * p):
                        pairs.append((i + j, i + j + k))
            k //= 2
        p *= 2
    return pairs


_SORT16 = _sort16_pairs()
_BITONIC16 = [(i, i + dd) for dd in (8, 4, 2, 1) for i in range(PEER_TOPK) if not i & dd]
_CELLS = [(i, j) for i in range(PEER_TOPK) for j in range(PEER_TOPK) if (i + 1) * (j + 1) <= PEER_TOPK]


def _exchange(vals, pairs):
    for a, b in pairs:
        hi, lo = jnp.maximum(vals[a], vals[b]), jnp.minimum(vals[a], vals[b])
        vals[a], vals[b] = hi, lo
    return vals


def _top16_sorted(slabs):
    groups = []
    for s in range(0, len(slabs), PEER_TOPK):
        groups.append(_exchange(list(slabs[s:s + PEER_TOPK]), _SORT16))
    while len(groups) > 1:
        merged = []
        for a, b in zip(groups[::2], groups[1::2]):
            c = [jnp.maximum(a[i], b[PEER_TOPK - 1 - i]) for i in range(PEER_TOPK)]
            merged.append(_exchange(c, _BITONIC16))
        groups = merged
    return groups[0]


def _peer_route_kernel(q_ref, ska0_ref, ska1_ref, skb1_ref, n_out, e1_out, r2_out, e2_out, s1_ref, s2_ref, v2_ref):
    qh = q_ref[...].astype(BF16)
    half = qh.shape[1] // 2
    s1_ref[...] = _dot_nt(ska0_ref[...], qh[:, :half])
    s2_ref[...] = _dot_nt(ska1_ref[...], qh[:, half:])
    nh = PEER_HEADS
    slab = lambda ref, n: ref[n * nh:(n + 1) * nh, :]

    v1 = _top16_sorted([slab(s1_ref, n) for n in range(N_KEYS)])
    v2 = _top16_sorted([slab(s2_ref, n) for n in range(N_KEYS)])

    cand = {c: v1[c[0]] + v2[c[1]] for c in _CELLS}
    cnt = {c: jnp.full_like(v1[0], float((c[0] + 1) * (c[1] + 1) - 1)) for c in _CELLS}
    for c1 in _CELLS:
        for c2 in _CELLS:
            if c1[0] < c2[0] and c1[1] > c2[1]:
                first = cand[c1] >= cand[c2]
                cnt[c2] = cnt[c2] + jnp.where(first, 1.0, 0.0)
                cnt[c1] = cnt[c1] + jnp.where(first, 0.0, 1.0)
    top = cand[(0, 0)]
    z = jnp.zeros_like(top)
    taken = [jnp.zeros_like(top) for _ in range(PEER_TOPK)]
    for c in _CELLS:
        sel = cnt[c] < float(PEER_TOPK)
        z = z + jnp.where(sel, jnp.exp(cand[c] - top), 0.0)
        taken[c[0]] = taken[c[0]] + jnp.where(sel, 1.0, 0.0)
    inv_z = 1.0 / z

    for n in range(N_KEYS):
        s = slab(s1_ref, n)
        cnt_a = jnp.zeros_like(s)
        for i in reversed(range(PEER_TOPK)):
            cnt_a = jnp.where(s == v1[i], taken[i], cnt_a)
        n_out[n * nh:(n + 1) * nh, :] = cnt_a
        e1_out[n * nh:(n + 1) * nh, :] = jnp.exp(s - v1[0]) * inv_z

    for j in range(PEER_TOPK):
        v2_ref[j] = v2[j]
    s2b = _dot_nt(skb1_ref[...], qh[:, half:])
    for h in range(nh):
        s = s2b[h * N_KEYS:(h + 1) * N_KEYS, :]
        rank = jnp.zeros_like(s)
        for j in range(PEER_TOPK):
            rank = jnp.where(v2_ref[j, h:h + 1, :] > s, float(j + 1), rank)
        r2_out[h * N_KEYS:(h + 1) * N_KEYS, :] = rank.astype(BF16)
        e2_out[h * N_KEYS:(h + 1) * N_KEYS, :] = jnp.exp(s - v2_ref[0, h:h + 1, :]).astype(BF16)


def _peer_route(q, ska0, ska1, skb1, tn):
    n, d = q.shape
    rows = PEER_HEADS * N_KEYS
    full = lambda a: pl.BlockSpec(a.shape, lambda i: (0,) * a.ndim)
    col = pl.BlockSpec((rows, tn), lambda i: (0, i))
    return pl.pallas_call(
        _peer_route_kernel,
        grid=(n // tn,),
        in_specs=[pl.BlockSpec((tn, d), lambda i: (i, 0)), full(ska0), full(ska1), full(skb1)],
        out_specs=[col] * 4,
        out_shape=[jax.ShapeDtypeStruct((rows, n), F32), jax.ShapeDtypeStruct((rows, n), F32),
                   jax.ShapeDtypeStruct((rows, n), BF16), jax.ShapeDtypeStruct((rows, n), BF16)],
        scratch_shapes=[pltpu.VMEM((rows, tn), F32), pltpu.VMEM((rows, tn), F32),
                        pltpu.VMEM((PEER_TOPK, PEER_HEADS, tn), F32)],
        compiler_params=_cparams(("arbitrary",)),
        name="peer_route",
    )(q, ska0, ska1, skb1)


def _peer_dense_kernel(n_eb, ct_ref, u_ref, vt_ref, n_ref, e1_ref, r2_ref, e2_ref, out_ref,
                       ht0_ref, ht1_ref, cf0_ref, cf1_ref, nb_ref, eb_ref):
    s = pl.program_id(0)

    @pl.when(s == 0)
    def _():
        for ref in (ht0_ref, ht1_ref, cf0_ref, cf1_ref):
            ref[...] = jnp.zeros_like(ref)

    @pl.when((s <= 2) | ((s - 2) % n_eb == 0))
    def _():
        out_ref[...] = jnp.zeros_like(out_ref)

    nh = PEER_HEADS
    tn = ct_ref.shape[1]
    rg = 16
    a_blk = u_ref.shape[0] // N_KEYS
    zero = jnp.zeros((rg, tn), BF16)
    g_half = N_KEYS // rg // 2

    def step(ht_new, ht_old, cf_old, cf_new):
        n_chunk = a_blk // 2
        e_rows, d_rows = u_ref.shape[0] // n_chunk, out_ref.shape[0] // n_chunk
        for ch in range(n_chunk):
            rows = slice(ch * e_rows, (ch + 1) * e_rows)
            ht_new[rows, :] = _dot(u_ref[rows, :], ct_ref[...])
            for r in range(2 * ch * nh, (2 * ch + 2) * nh):
                for cols in [slice(lt * LANES, (lt + 1) * LANES) for lt in range(tn // LANES)]:
                    for src, dst in ((n_ref, nb_ref), (e1_ref, eb_ref)):
                        dst[r * rg:(r + 1) * rg, cols] = jnp.broadcast_to(src[r:r + 1, cols], (rg, LANES)).astype(BF16)
            for al, g0 in [(a, g) for a in range(2 * ch, 2 * ch + 2) for g in (0, g_half)]:
                gate = [zero] * g_half
                for h in range(nh):
                    r = al * nh + h
                    n_a, e1_a = nb_ref[r * rg:(r + 1) * rg, :], eb_ref[r * rg:(r + 1) * rg, :]
                    for g in range(g_half):
                        rows = slice(h * N_KEYS + (g0 + g) * rg, h * N_KEYS + (g0 + g + 1) * rg)
                        gate[g] = gate[g] + jnp.where(r2_ref[rows, :] < n_a, e2_ref[rows, :], zero) * e1_a
                for g in range(g_half):
                    rows = slice(al * N_KEYS + (g0 + g) * rg, al * N_KEYS + (g0 + g + 1) * rg)
                    ht = ht_old[rows, :].astype(BF16)
                    act = ht * (0.5 * (1.0 + lax.erf(ht * (2.0 ** -0.5))))
                    cf_old[rows, :] = act * gate[g]
            rows = slice(ch * d_rows, (ch + 1) * d_rows)
            out_ref[rows, :] += _dot(vt_ref[rows, :], cf_new[...])

    @pl.when(s % 2 == 0)
    def _():
        step(ht0_ref, ht1_ref, cf1_ref, cf0_ref)

    @pl.when(s % 2 == 1)
    def _():
        step(ht1_ref, ht0_ref, cf0_ref, cf1_ref)


def _peer_dense(ct, u_bf, vt_bf, n_a, e1, r2, e2, tn, a_blk):
    d, n = ct.shape
    e_blk = a_blk * N_KEYS
    n_eb = u_bf.shape[0] // e_blk
    last = (n // tn) * n_eb - 1
    rows = PEER_HEADS * N_KEYS
    pair = lambda s, lag: jnp.clip(s - lag, 0, last)
    tok = lambda s, lag: pair(s, lag) // n_eb
    blk = lambda s, lag: pair(s, lag) % n_eb
    return pl.pallas_call(
        functools.partial(_peer_dense_kernel, n_eb),
        grid=(last + 3,),
        in_specs=[pl.BlockSpec((d, tn), lambda s: (0, tok(s, 0))),
                  pl.BlockSpec((e_blk, d), lambda s: (blk(s, 0), 0)),
                  pl.BlockSpec((d, e_blk), lambda s: (0, blk(s, 2))),
                  pl.BlockSpec((a_blk * PEER_HEADS, tn), lambda s: (blk(s, 1), tok(s, 1))),
                  pl.BlockSpec((a_blk * PEER_HEADS, tn), lambda s: (blk(s, 1), tok(s, 1))),
                  pl.BlockSpec((rows, tn), lambda s: (0, tok(s, 1))),
                  pl.BlockSpec((rows, tn), lambda s: (0, tok(s, 1)))],
        out_specs=pl.BlockSpec((d, tn), lambda s: (0, tok(s, 2))),
        out_shape=jax.ShapeDtypeStruct((d, n), F32),
        scratch_shapes=([pltpu.VMEM((e_blk, tn), F32)] * 2 + [pltpu.VMEM((e_blk, tn), BF16)] * 2
                        + [pltpu.VMEM((a_blk * PEER_HEADS * 16, tn), BF16)] * 2),
        compiler_params=_cparams(("arbitrary",)),
        name="peer_dense",
    )(ct, u_bf, vt_bf, n_a, e1, r2, e2)


def _final_kernel(last_layer, h1_ref, ffnt_ref, p_ref, png_ref, wg_ref, wp_ref, fng_ref, out_ref):
    h2 = h1_ref[...] + ffnt_ref[...].T
    gate = jax.nn.sigmoid(_dot(_rms(h2, png_ref[...]).astype(BF16), wg_ref[...]))
    h3 = h2 + gate * _dot(p_ref[...].astype(BF16), wp_ref[...])
    out_ref[...] = _rms(h3, fng_ref[...]) if last_layer else h3


def _final(h1, ffnt, p, png, wg, wp, fng, tm, last_layer):
    n, d = h1.shape
    full = lambda a: pl.BlockSpec(a.shape, lambda i: (0,) * a.ndim)
    return pl.pallas_call(
        functools.partial(_final_kernel, last_layer),
        grid=(n // tm,),
        in_specs=[pl.BlockSpec((tm, d), lambda i: (i, 0)), pl.BlockSpec((d, tm), lambda i: (0, i)),
                  pl.BlockSpec((tm, p.shape[1]), lambda i: (i, 0)), full(png), full(wg), full(wp), full(fng)],
        out_specs=pl.BlockSpec((tm, d), lambda i: (i, 0)),
        out_shape=jax.ShapeDtypeStruct((n, d), F32),
        compiler_params=_cparams(("arbitrary",)),
        name="final",
    )(h1, ffnt, p, png, wg, wp, fng)


def _tile(n, want):
    t = min(n, want)
    assert n % t == 0, (n, t)
    return t


def kernel(x, p, norm_mix_g, w_in, rwkv_mu, rwkv_w0, rwkv_w_up, rwkv_a0, rwkv_a_up, rwkv_g_up, rwkv_k_k, rwkv_k_a, rwkv_r_k, rwkv_lnx_g, rwkv_lnx_b, rwkv_w_o, conv_dw_w, conv_dw_b, conv_ln_g, conv_ln_b, conv_w_o, w_out, norm_ffn_g, peer_w_q, peer_sub_keys, peer_u, peer_v, ple_w_proj, ple_norm_g, ple_w_gate, final_norm_g):
    bsz, seq, d = x.shape
    n = bsz * seq
    nh, hd = RWKV_HEADS, RWKV_HEAD_DIM
    row = lambda a: a.reshape(1, -1)
    head_of = np.arange(RWKV_WIDTH) // hd
    seg = jnp.asarray(head_of[:, None] == head_of[None, :], BF16)
    h = x.reshape(n, d)

    for l in range(w_in.shape[0]):
        zr, hg, sg = _in_proj(h, row(norm_mix_g[l]), w_in[l].astype(BF16), _tile(seq, 512))
        r, w, k, v, nkk, kka, g, bonus = _rwkv_prep(
            zr, row(rwkv_mu[l]), row(rwkv_w0[l]), rwkv_w_up[l], row(rwkv_a0[l]), rwkv_a_up[l], rwkv_g_up[l],
            row(rwkv_k_k[l]), row(rwkv_k_a[l]), row(rwkv_r_k[l]), seg, _tile(seq, 512), seq)

        heads = lambda a: a.reshape(bsz, seq, nh, hd)
        y = _rwkv_scan(heads(r), heads(w), heads(k), heads(v), heads(kka), heads(nkk), _tile(seq, 16))
        y = y.reshape(n, RWKV_WIDTH)

        consts = (row(rwkv_lnx_g[l]), row(rwkv_lnx_b[l]), rwkv_w_o[l].astype(BF16), conv_dw_w[l],
                  row(conv_dw_b[l]), row(conv_ln_g[l]), row(conv_ln_b[l]), conv_w_o[l].astype(BF16),
                  w_out[l].astype(BF16), row(norm_ffn_g[l]),
                  peer_w_q[l].reshape(d, PEER_HEADS, 2, PEER_HALF).transpose(0, 2, 1, 3).reshape(d, -1).astype(BF16))
        h1, ct, q = _mix_out(h, y, g, bonus, hg, sg, seg, consts, _tile(seq, 256), seq)

        sk = peer_sub_keys[l]
        eye = jnp.eye(PEER_HEADS, dtype=sk.dtype)
        blk = lambda half, order: jnp.einsum("hnd,hg->" + order, sk[:, half], eye).reshape(
            PEER_HEADS * N_KEYS, PEER_HEADS * PEER_HALF).astype(BF16)
        n_a, e1, r2, e2 = _peer_route(q, blk(0, "nhgd"), blk(1, "nhgd"), blk(1, "hngd"), _tile(n, 256))
        ffnt = _peer_dense(ct, peer_u[l].astype(BF16), peer_v[l].T.astype(BF16), n_a, e1, r2, e2,
                           _tile(n, 512), 8)
        h = _final(h1, ffnt, p[l].reshape(n, -1), row(ple_norm_g[l]), ple_w_gate[l].astype(BF16),
                   ple_w_proj[l].astype(BF16), row(final_norm_g), _tile(seq, 256), l == w_in.shape[0] - 1)
    return h.reshape(bsz, seq, d)
```

```python
import functools

import numpy as np
import jax
import jax.numpy as jnp
from jax import lax
from jax.experimental import pallas as pl
from jax.experimental.pallas import tpu as pltpu

F32 = jnp.float32
BF16 = jnp.bfloat16

RWKV_HEADS = 8
RWKV_HEAD_DIM = 64
RWKV_WIDTH = RWKV_HEADS * RWKV_HEAD_DIM
DECAY_LORA = 64
AAA_LORA = 64
GATE_LORA = 128
RWKV_COLS = 3 * RWKV_WIDTH + DECAY_LORA + AAA_LORA + GATE_LORA
GN_EPS = 64e-5
CONV_WIDTH = 512
CONV_K = 31
CONV_HALO = 32
LN_EPS = 1e-5
RMS_EPS = 1e-6
PEER_HEADS = 8
N_KEYS = 128
PEER_HALF = 64
PEER_TOPK = 16

VMEM_LIMIT = 56 * 1024 * 1024
LANES = 128
SUBLANES = 8


def _cparams(sem):
    return pltpu.CompilerParams(dimension_semantics=sem, vmem_limit_bytes=VMEM_LIMIT)


def _dot(a, b):
    return jnp.dot(a, b, preferred_element_type=F32)


def _dot_nt(a, b):
    return lax.dot_general(a, b, (((1,), (1,)), ((), ())), preferred_element_type=F32)


def _dot_split(x, m_bf16):
    hi = x.astype(BF16)
    lo = (x - hi.astype(F32)).astype(BF16)
    return _dot(hi, m_bf16) + _dot(lo, m_bf16)


def _rms(x, g):
    return x * lax.rsqrt(jnp.mean(x * x, axis=-1, keepdims=True) + RMS_EPS) * g


def _in_proj_kernel(x_ref, g_ref, w_ref, zr_ref, hg_ref, sg_ref):
    a = _rms(x_ref[...], g_ref[...]).astype(BF16)
    c0, c1, c2 = RWKV_COLS, RWKV_COLS + CONV_WIDTH, RWKV_COLS + 2 * CONV_WIDTH
    zr_ref[...] = _dot(a, w_ref[:, :c0])
    u = _dot(a, w_ref[:, c0:c1])
    gate = _dot(a, w_ref[:, c1:c2])
    hg_ref[...] = u * jax.nn.sigmoid(gate)
    sg_ref[...] = jax.nn.sigmoid(_dot(a, w_ref[:, c2:])).astype(BF16)


def _in_proj(x, g, w_in, tm):
    n, d = x.shape
    cols = w_in.shape[1]
    gate_cols = cols - RWKV_COLS - 2 * CONV_WIDTH
    row = lambda w: pl.BlockSpec((tm, w), lambda i: (i, 0))
    full = lambda a: pl.BlockSpec(a.shape, lambda i: (0,) * a.ndim)
    return pl.pallas_call(
        _in_proj_kernel,
        grid=(n // tm,),
        in_specs=[row(d), full(g), full(w_in)],
        out_specs=[row(RWKV_COLS), row(CONV_WIDTH), row(gate_cols)],
        out_shape=[jax.ShapeDtypeStruct((n, RWKV_COLS), F32),
                   jax.ShapeDtypeStruct((n, CONV_WIDTH), F32),
                   jax.ShapeDtypeStruct((n, gate_cols), BF16)],
        compiler_params=_cparams(("arbitrary",)),
        name="in_proj",
    )(x, g, w_in)


def _rwkv_prep_kernel(tiles_per_seq, z_ref, mu_ref, w0_ref, wup_ref, a0_ref, aup_ref, gup_ref,
                      kk_ref, ka_ref, rk_ref, seg_ref,
                      r_out, w_out, k_out, v_out, nkk_out, kka_out, g_out, bonus_out, last_ref):
    i = pl.program_id(0)

    @pl.when(i % tiles_per_seq == 0)
    def _():
        last_ref[...] = jnp.zeros_like(last_ref)

    z = z_ref[...]
    tm = z.shape[0]
    rows = lax.broadcasted_iota(jnp.int32, z.shape, 0)
    zs = jnp.where(rows == 0, last_ref[...], pltpu.roll(z, 1, axis=0))
    last_ref[...] = z[tm - 1:tm, :]
    z = z + (zs - z) * mu_ref[...]

    s0 = RWKV_WIDTH
    r = z[:, :s0]
    k = z[:, s0:2 * s0]
    v = z[:, 2 * s0:3 * s0]
    o = 3 * s0
    wl = z[:, o:o + DECAY_LORA]
    al = z[:, o + DECAY_LORA:o + DECAY_LORA + AAA_LORA]
    gl = z[:, o + DECAY_LORA + AAA_LORA:]

    hp = lax.Precision.HIGHEST
    w_log = -jax.nn.softplus(-(w0_ref[...] + jnp.dot(jnp.tanh(wl), wup_ref[...], precision=hp))) - 0.5
    decay = jnp.exp(-jnp.exp(w_log))
    a = jax.nn.sigmoid(a0_ref[...] + jnp.dot(al, aup_ref[...], precision=hp))
    g = jnp.dot(jax.nn.sigmoid(gl), gup_ref[...], precision=hp)

    seg = seg_ref[...]
    kk = k * kk_ref[...]
    kk = kk * lax.rsqrt(jnp.maximum(_dot_split(kk * kk, seg), 1e-24))
    k2 = k * (1.0 + (a - 1.0) * ka_ref[...])
    bonus = _dot_split(r * k2 * rk_ref[...], seg) * v

    r_out[...] = r
    w_out[...] = decay
    k_out[...] = k2
    v_out[...] = v
    nkk_out[...] = -kk
    kka_out[...] = kk * a
    g_out[...] = g
    bonus_out[...] = bonus


def _rwkv_prep(zr, mu, w0, w_up, a0, a_up, g_up, k_k, k_a, r_k, seg, tm, seq):
    n = zr.shape[0]
    row = lambda w: pl.BlockSpec((tm, w), lambda i: (i, 0))
    full = lambda a: pl.BlockSpec(a.shape, lambda i: (0,) * a.ndim)
    consts = (mu, w0, w_up, a0, a_up, g_up, k_k, k_a, r_k, seg)
    return pl.pallas_call(
        functools.partial(_rwkv_prep_kernel, seq // tm),
        grid=(n // tm,),
        in_specs=[row(RWKV_COLS)] + [full(c) for c in consts],
        out_specs=[row(RWKV_WIDTH)] * 8,
        out_shape=[jax.ShapeDtypeStruct((n, RWKV_WIDTH), F32)] * 8,
        scratch_shapes=[pltpu.VMEM((1, RWKV_COLS), F32)],
        compiler_params=_cparams(("arbitrary",)),
        name="rwkv_prep",
    )(zr, *consts)


def _rwkv_scan_kernel(r_ref, w_ref, k_ref, v_ref, kka_ref, nkk_ref, y_ref, s_ref, sa_ref, yt_ref, rows_ref, next_ref):
    @pl.when(pl.program_id(0) == 0)
    def _():
        s_ref[...] = jnp.zeros_like(s_ref)
        yt_ref[...] = jnp.zeros_like(yt_ref)

    bsz, steps, width = r_ref.shape
    nh, hd = RWKV_HEADS, RWKV_HEAD_DIM
    chains = bsz * nh

    def to_lanes(ref, t):
        x = ref[:, t, :]
        pairs = jnp.concatenate([x[:, j * LANES:(j + 1) * LANES] for j in range(width // LANES)], axis=0)
        xt = pairs.T
        return jnp.concatenate([xt[:hd, :], xt[hd:, :]], axis=1)

    w_i, k_i, kka_i, r_i, nkk_i, v_i = range(6)

    def stage(t):
        t = jnp.minimum(t, steps - 1)
        for i, ref in ((w_i, w_ref), (k_i, k_ref), (kka_i, kka_ref), (r_i, r_ref), (v_i, v_ref)):
            next_ref[i] = to_lanes(ref, t)
        next_ref[nkk_i] = to_lanes(nkk_ref, jnp.minimum(t + 1, steps - 1))

    def emit(t):
        y = yt_ref[...]
        yt = jnp.concatenate([y[:, :chains // 2], y[:, chains // 2:]], axis=0).T
        y_ref[:, t, :] = jnp.concatenate([yt[j * bsz:(j + 1) * bsz, :] for j in range(width // LANES)], axis=1)

    rows_ref[nkk_i] = to_lanes(nkk_ref, 0)
    part = [jnp.zeros((hd, chains), F32) for _ in range(4)]
    for kx in range(hd):
        part[kx % 4] = part[kx % 4] + s_ref[kx] * rows_ref[nkk_i, pl.ds(kx, 1), :]
    sa_ref[...] = (part[0] + part[1]) + (part[2] + part[3])
    stage(0)
    rows_ref[...] = next_ref[...]

    def step(t, carry):
        row = lambda i, kx: rows_ref[i, pl.ds(kx, 1), :]
        v_t, sa_old = rows_ref[v_i], sa_ref[...]
        y = [jnp.zeros_like(v_t), jnp.zeros_like(v_t)]
        sa = [jnp.zeros_like(v_t), jnp.zeros_like(v_t)]
        for kx in range(hd):
            s_k = s_ref[kx] * row(w_i, kx) + sa_old * row(kka_i, kx) + v_t * row(k_i, kx)
            s_ref[kx] = s_k
            y[kx % 2] = y[kx % 2] + s_k * row(r_i, kx)
            sa[kx % 2] = sa[kx % 2] + s_k * row(nkk_i, kx)
        sa_ref[...] = sa[0] + sa[1]
        emit(jnp.maximum(t - 1, 0))
        yt_ref[...] = y[0] + y[1]
        stage(t + 1)
        rows_ref[...] = next_ref[...]
        return carry

    lax.fori_loop(0, steps, step, 0)
    emit(steps - 1)


def _rwkv_scan(r, w, k, v, kka, nkk, steps):
    bsz, t, width = r.shape
    nh, hd = RWKV_HEADS, RWKV_HEAD_DIM
    blk = pl.BlockSpec((bsz, steps, width), lambda i: (0, i, 0))
    return pl.pallas_call(
        _rwkv_scan_kernel,
        grid=(t // steps,),
        in_specs=[blk] * 6,
        out_specs=blk,
        out_shape=jax.ShapeDtypeStruct((bsz, t, width), F32),
        scratch_shapes=[pltpu.VMEM((hd, hd, bsz * nh), F32), pltpu.VMEM((hd, bsz * nh), F32),
                        pltpu.VMEM((hd, bsz * nh), F32), pltpu.VMEM((6, hd, bsz * nh), F32),
                        pltpu.VMEM((6, hd, bsz * nh), F32)],
        compiler_params=_cparams(("arbitrary",)),
        name="rwkv_scan",
    )(r, w, k, v, kka, nkk)


def _mix_out_kernel(tiles_per_seq, x_ref, y_ref, g_ref, bonus_ref, hg_ref, sg_ref, seg_ref,
                    lnxg_ref, lnxb_ref, wor_ref, dww_ref, dwb_ref, clng_ref, clnb_ref, woc_ref,
                    wout_ref, nfg_ref, wq_ref, h1_out, ct_out, q_out, buf_ref):
    i = pl.program_id(0)
    tm = x_ref.shape[0]

    @pl.when(i % tiles_per_seq == 0)
    def _():
        buf_ref[:CONV_HALO, :] = jnp.zeros((CONV_HALO, CONV_WIDTH), F32)
        buf_ref[CONV_HALO + tm:, :] = jnp.zeros((SUBLANES, CONV_WIDTH), F32)

    seg = seg_ref[...]
    y = y_ref[...]
    inv_n = 1.0 / RWKV_HEAD_DIM
    mean = _dot_split(y, seg) * inv_n
    yc = y - mean
    var = _dot_split(yc * yc, seg) * inv_n
    yn = yc * lax.rsqrt(var + GN_EPS) * lnxg_ref[...] + lnxb_ref[...]
    o_r = _dot(((yn + bonus_ref[...]) * g_ref[...]).astype(BF16), wor_ref[...])

    buf_ref[CONV_HALO:CONV_HALO + tm, :] = hg_ref[...]
    acc = jnp.zeros((tm, CONV_WIDTH), F32) + dwb_ref[...]
    for phase in range(SUBLANES):
        part = None
        for j in range(CONV_K):
            off = CONV_HALO - (CONV_K - 1) + j
            if off % SUBLANES == phase:
                base = off - phase
                term = buf_ref[base:base + tm + SUBLANES, :] * dww_ref[j:j + 1, :]
                part = term if part is None else part + term
        acc = acc + part[phase:phase + tm, :]
    buf_ref[:CONV_HALO, :] = buf_ref[tm:tm + CONV_HALO, :]
    mu = jnp.mean(acc, axis=-1, keepdims=True)
    ac = acc - mu
    cv = jnp.mean(ac * ac, axis=-1, keepdims=True)
    hf = ac * lax.rsqrt(cv + LN_EPS) * clng_ref[...] + clnb_ref[...]
    o_c = _dot((hf * jax.nn.sigmoid(hf)).astype(BF16), woc_ref[...])

    d = x_ref.shape[1]
    sg = sg_ref[...].astype(F32)
    mix = sg[:, :d] * o_r + sg[:, d:] * o_c
    h1 = x_ref[...] + _dot(mix.astype(BF16), wout_ref[...])
    h1_out[...] = h1
    c = _rms(h1, nfg_ref[...])
    ct_out[...] = c.T.astype(BF16)
    q_out[...] = _dot(c.astype(BF16), wq_ref[...])


def _mix_out(x, y, g, bonus, hg, sg, seg, consts, tm, seq):
    n, d = x.shape
    row = lambda w: pl.BlockSpec((tm, w), lambda i: (i, 0))
    full = lambda a: pl.BlockSpec(a.shape, lambda i: (0,) * a.ndim)
    return pl.pallas_call(
        functools.partial(_mix_out_kernel, seq // tm),
        grid=(n // tm,),
        in_specs=[row(d), row(RWKV_WIDTH), row(RWKV_WIDTH), row(RWKV_WIDTH), row(CONV_WIDTH), row(2 * d),
                  full(seg)] + [full(c) for c in consts],
        out_specs=[row(d), pl.BlockSpec((d, tm), lambda i: (0, i)), row(d)],
        out_shape=[jax.ShapeDtypeStruct((n, d), F32), jax.ShapeDtypeStruct((d, n), BF16),
                   jax.ShapeDtypeStruct((n, d), F32)],
        scratch_shapes=[pltpu.VMEM((tm + CONV_HALO + SUBLANES, CONV_WIDTH), F32)],
        compiler_params=_cparams(("arbitrary",)),
        name="mix_out",
    )(x, y, g, bonus, hg, sg, seg, *consts)


def _sort16_pairs():
    n, pairs, p = PEER_TOPK, [], 1
    while p < n:
        k = p
        while k >= 1:
            for j in range(k % p, n - k, 2 * k):
                for i in range(min(k, n - j - k)):
                    if (i + j) // (2 * p) == (i + j + k) // (2 * p):
                        pairs.append((i + j, i + j + k))
            k //= 2
        p *= 2
    return pairs


_SORT16 = _sort16_pairs()
_BITONIC16 = [(i, i + dd) for dd in (8, 4, 2, 1) for i in range(PEER_TOPK) if not i & dd]
_CELLS = [(i, j) for i in range(PEER_TOPK) for j in range(PEER_TOPK) if (i + 1) * (j + 1) <= PEER_TOPK]


def _exchange(vals, pairs):
    for a, b in pairs:
        hi, lo = jnp.maximum(vals[a], vals[b]), jnp.minimum(vals[a], vals[b])
        vals[a], vals[b] = hi, lo
    return vals


def _top16_sorted(slabs):
    groups = []
    for s in range(0, len(slabs), PEER_TOPK):
        groups.append(_exchange(list(slabs[s:s + PEER_TOPK]), _SORT16))
    while len(groups) > 1:
        merged = []
        for a, b in zip(groups[::2], groups[1::2]):
            c = [jnp.maximum(a[i], b[PEER_TOPK - 1 - i]) for i in range(PEER_TOPK)]
            merged.append(_exchange(c, _BITONIC16))
        groups = merged
    return groups[0]


def _peer_route_kernel(q_ref, ska0_ref, ska1_ref, skb1_ref, n_out, e1_out, r2_out, e2_out, s1_ref, s2_ref, v2_ref):
    qh = q_ref[...].astype(BF16)
    half = qh.shape[1] // 2
    s1_ref[...] = _dot_nt(ska0_ref[...], qh[:, :half])
    s2_ref[...] = _dot_nt(ska1_ref[...], qh[:, half:])
    nh = PEER_HEADS
    slab = lambda ref, n: ref[n * nh:(n + 1) * nh, :]

    v1 = _top16_sorted([slab(s1_ref, n) for n in range(N_KEYS)])
    v2 = _top16_sorted([slab(s2_ref, n) for n in range(N_KEYS)])

    cand = {c: v1[c[0]] + v2[c[1]] for c in _CELLS}
    cnt = {c: jnp.full_like(v1[0], float((c[0] + 1) * (c[1] + 1) - 1)) for c in _CELLS}
    for c1 in _CELLS:
        for c2 in _CELLS:
            if c1[0] < c2[0] and c1[1] > c2[1]:
                first = cand[c1] >= cand[c2]
                cnt[c2] = cnt[c2] + jnp.where(first, 1.0, 0.0)
                cnt[c1] = cnt[c1] + jnp.where(first, 0.0, 1.0)
    top = cand[(0, 0)]
    z = jnp.zeros_like(top)
    taken = [jnp.zeros_like(top) for _ in range(PEER_TOPK)]
    for c in _CELLS:
        sel = cnt[c] < float(PEER_TOPK)
        z = z + jnp.where(sel, jnp.exp(cand[c] - top), 0.0)
        taken[c[0]] = taken[c[0]] + jnp.where(sel, 1.0, 0.0)
    inv_z = 1.0 / z

    for n in range(N_KEYS):
        s = slab(s1_ref, n)
        cnt_a = jnp.zeros_like(s)
        for i in reversed(range(PEER_TOPK)):
            cnt_a = jnp.where(s == v1[i], taken[i], cnt_a)
        n_out[n * nh:(n + 1) * nh, :] = cnt_a
        e1_out[n * nh:(n + 1) * nh, :] = jnp.exp(s - v1[0]) * inv_z

    for j in range(PEER_TOPK):
        v2_ref[j] = v2[j]
    s2b = _dot_nt(skb1_ref[...], qh[:, half:])
    for h in range(nh):
        s = s2b[h * N_KEYS:(h + 1) * N_KEYS, :]
        rank = jnp.zeros_like(s)
        for j in range(PEER_TOPK):
            rank = jnp.where(v2_ref[j, h:h + 1, :] > s, float(j + 1), rank)
        r2_out[h * N_KEYS:(h + 1) * N_KEYS, :] = rank.astype(BF16)
        e2_out[h * N_KEYS:(h + 1) * N_KEYS, :] = jnp.exp(s - v2_ref[0, h:h + 1, :]).astype(BF16)


def _peer_route(q, ska0, ska1, skb1, tn):
    n, d = q.shape
    rows = PEER_HEADS * N_KEYS
    full = lambda a: pl.BlockSpec(a.shape, lambda i: (0,) * a.ndim)
    col = pl.BlockSpec((rows, tn), lambda i: (0, i))
    return pl.pallas_call(
        _peer_route_kernel,
        grid=(n // tn,),
        in_specs=[pl.BlockSpec((tn, d), lambda i: (i, 0)), full(ska0), full(ska1), full(skb1)],
        out_specs=[col] * 4,
        out_shape=[jax.ShapeDtypeStruct((rows, n), F32), jax.ShapeDtypeStruct((rows, n), F32),
                   jax.ShapeDtypeStruct((rows, n), BF16), jax.ShapeDtypeStruct((rows, n), BF16)],
        scratch_shapes=[pltpu.VMEM((rows, tn), F32), pltpu.VMEM((rows, tn), F32),
                        pltpu.VMEM((PEER_TOPK, PEER_HEADS, tn), F32)],
        compiler_params=_cparams(("arbitrary",)),
        name="peer_route",
    )(q, ska0, ska1, skb1)


def _peer_dense_kernel(n_eb, ct_ref, u_ref, vt_ref, n_ref, e1_ref, r2_ref, e2_ref, out_ref,
                       ht0_ref, ht1_ref, cf0_ref, cf1_ref, nb_ref, eb_ref):
    s = pl.program_id(0)

    @pl.when(s == 0)
    def _():
        for ref in (ht0_ref, ht1_ref, cf0_ref, cf1_ref):
            ref[...] = jnp.zeros_like(ref)

    @pl.when((s <= 2) | ((s - 2) % n_eb == 0))
    def _():
        out_ref[...] = jnp.zeros_like(out_ref)

    nh = PEER_HEADS
    tn = ct_ref.shape[1]
    rg = 16
    a_blk = u_ref.shape[0] // N_KEYS
    zero = jnp.zeros((rg, tn), BF16)
    g_half = N_KEYS // rg // 2

    def step(ht_new, ht_old, cf_old, cf_new):
        n_chunk = a_blk // 2
        e_rows, d_rows = u_ref.shape[0] // n_chunk, out_ref.shape[0] // n_chunk
        for ch in range(n_chunk):
            rows = slice(ch * e_rows, (ch + 1) * e_rows)
            ht_new[rows, :] = _dot(u_ref[rows, :], ct_ref[...])
            for r in range(2 * ch * nh, (2 * ch + 2) * nh):
                for cols in [slice(lt * LANES, (lt + 1) * LANES) for lt in range(tn // LANES)]:
                    for src, dst in ((n_ref, nb_ref), (e1_ref, eb_ref)):
                        dst[r * rg:(r + 1) * rg, cols] = jnp.broadcast_to(src[r:r + 1, cols], (rg, LANES)).astype(BF16)
            for al, g0 in [(a, g) for a in range(2 * ch, 2 * ch + 2) for g in (0, g_half)]:
                gate = [zero] * g_half
                for h in range(nh):
                    r = al * nh + h
                    n_a, e1_a = nb_ref[r * rg:(r + 1) * rg, :], eb_ref[r * rg:(r + 1) * rg, :]
                    for g in range(g_half):
                        rows = slice(h * N_KEYS + (g0 + g) * rg, h * N_KEYS + (g0 + g + 1) * rg)
                        gate[g] = gate[g] + jnp.where(r2_ref[rows, :] < n_a, e2_ref[rows, :], zero) * e1_a
                for g in range(g_half):
                    rows = slice(al * N_KEYS + (g0 + g) * rg, al * N_KEYS + (g0 + g + 1) * rg)
                    ht = ht_old[rows, :].astype(BF16)
                    act = ht * (0.5 * (1.0 + lax.erf(ht * (2.0 ** -0.5))))
                    cf_old[rows, :] = act * gate[g]
            rows = slice(ch * d_rows, (ch + 1) * d_rows)
            out_ref[rows, :] += _dot(vt_ref[rows, :], cf_new[...])

    @pl.when(s % 2 == 0)
    def _():
        step(ht0_ref, ht1_ref, cf1_ref, cf0_ref)

    @pl.when(s % 2 == 1)
    def _():
        step(ht1_ref, ht0_ref, cf0_ref, cf1_ref)


def _peer_dense(ct, u_bf, vt_bf, n_a, e1, r2, e2, tn, a_blk):
    d, n = ct.shape
    e_blk = a_blk * N_KEYS
    n_eb = u_bf.shape[0] // e_blk
    last = (n // tn) * n_eb - 1
    rows = PEER_HEADS * N_KEYS
    pair = lambda s, lag: jnp.clip(s - lag, 0, last)
    tok = lambda s, lag: pair(s, lag) // n_eb
    blk = lambda s, lag: pair(s, lag) % n_eb
    return pl.pallas_call(
        functools.partial(_peer_dense_kernel, n_eb),
        grid=(last + 3,),
        in_specs=[pl.BlockSpec((d, tn), lambda s: (0, tok(s, 0))),
                  pl.BlockSpec((e_blk, d), lambda s: (blk(s, 0), 0)),
                  pl.BlockSpec((d, e_blk), lambda s: (0, blk(s, 2))),
                  pl.BlockSpec((a_blk * PEER_HEADS, tn), lambda s: (blk(s, 1), tok(s, 1))),
                  pl.BlockSpec((a_blk * PEER_HEADS, tn), lambda s: (blk(s, 1), tok(s, 1))),
                  pl.BlockSpec((rows, tn), lambda s: (0, tok(s, 1))),
                  pl.BlockSpec((rows, tn), lambda s: (0, tok(s, 1)))],
        out_specs=pl.BlockSpec((d, tn), lambda s: (0, tok(s, 2))),
        out_shape=jax.ShapeDtypeStruct((d, n), F32),
        scratch_shapes=([pltpu.VMEM((e_blk, tn), F32)] * 2 + [pltpu.VMEM((e_blk, tn), BF16)] * 2
                        + [pltpu.VMEM((a_blk * PEER_HEADS * 16, tn), BF16)] * 2),
        compiler_params=_cparams(("arbitrary",)),
        name="peer_dense",
    )(ct, u_bf, vt_bf, n_a, e1, r2, e2)


def _final_kernel(last_layer, h1_ref, ffnt_ref, p_ref, png_ref, wg_ref, wp_ref, fng_ref, out_ref):
    h2 = h1_ref[...] + ffnt_ref[...].T
    gate = jax.nn.sigmoid(_dot(_rms(h2, png_ref[...]).astype(BF16), wg_ref[...]))
    h3 = h2 + gate * _dot(p_ref[...].astype(BF16), wp_ref[...])
    out_ref[...] = _rms(h3, fng_ref[...]) if last_layer else h3


def _final(h1, ffnt, p, png, wg, wp, fng, tm, last_layer):
    n, d = h1.shape
    full = lambda a: pl.BlockSpec(a.shape, lambda i: (0,) * a.ndim)
    return pl.pallas_call(
        functools.partial(_final_kernel, last_layer),
        grid=(n // tm,),
        in_specs=[pl.BlockSpec((tm, d), lambda i: (i, 0)), pl.BlockSpec((d, tm), lambda i: (0, i)),
                  pl.BlockSpec((tm, p.shape[1]), lambda i: (i, 0)), full(png), full(wg), full(wp), full(fng)],
        out_specs=pl.BlockSpec((tm, d), lambda i: (i, 0)),
        out_shape=jax.ShapeDtypeStruct((n, d), F32),
        compiler_params=_cparams(("arbitrary",)),
        name="final",
    )(h1, ffnt, p, png, wg, wp, fng)


def _tile(n, want):
    t = min(n, want)
    assert n % t == 0, (n, t)
    return t


def kernel(x, p, norm_mix_g, w_in, rwkv_mu, rwkv_w0, rwkv_w_up, rwkv_a0, rwkv_a_up, rwkv_g_up, rwkv_k_k, rwkv_k_a, rwkv_r_k, rwkv_lnx_g, rwkv_lnx_b, rwkv_w_o, conv_dw_w, conv_dw_b, conv_ln_g, conv_ln_b, conv_w_o, w_out, norm_ffn_g, peer_w_q, peer_sub_keys, peer_u, peer_v, ple_w_proj, ple_norm_g, ple_w_gate, final_norm_g):
    bsz, seq, d = x.shape
    n = bsz * seq
    nh, hd = RWKV_HEADS, RWKV_HEAD_DIM
    row = lambda a: a.reshape(1, -1)
    head_of = np.arange(RWKV_WIDTH) // hd
    seg = jnp.asarray(head_of[:, None] == head_of[None, :], BF16)
    h = x.reshape(n, d)

    for l in range(w_in.shape[0]):
        zr, hg, sg = _in_proj(h, row(norm_mix_g[l]), w_in[l].astype(BF16), _tile(seq, 512))
        r, w, k, v, nkk, kka, g, bonus = _rwkv_prep(
            zr, row(rwkv_mu[l]), row(rwkv_w0[l]), rwkv_w_up[l], row(rwkv_a0[l]), rwkv_a_up[l], rwkv_g_up[l],
            row(rwkv_k_k[l]), row(rwkv_k_a[l]), row(rwkv_r_k[l]), seg, _tile(seq, 512), seq)

        seqs = lambda a: a.reshape(bsz, seq, RWKV_WIDTH)
        y = _rwkv_scan(seqs(r), seqs(w), seqs(k), seqs(v), seqs(kka), seqs(nkk), _tile(seq, 16))
        y = y.reshape(n, RWKV_WIDTH)

        consts = (row(rwkv_lnx_g[l]), row(rwkv_lnx_b[l]), rwkv_w_o[l].astype(BF16), conv_dw_w[l],
                  row(conv_dw_b[l]), row(conv_ln_g[l]), row(conv_ln_b[l]), conv_w_o[l].astype(BF16),
                  w_out[l].astype(BF16), row(norm_ffn_g[l]),
                  peer_w_q[l].reshape(d, PEER_HEADS, 2, PEER_HALF).transpose(0, 2, 1, 3).reshape(d, -1).astype(BF16))
        h1, ct, q = _mix_out(h, y, g, bonus, hg, sg, seg, consts, _tile(seq, 256), seq)

        sk = peer_sub_keys[l]
        eye = jnp.eye(PEER_HEADS, dtype=sk.dtype)
        blk = lambda half, order: jnp.einsum("hnd,hg->" + order, sk[:, half], eye).reshape(
            PEER_HEADS * N_KEYS, PEER_HEADS * PEER_HALF).astype(BF16)
        n_a, e1, r2, e2 = _peer_route(q, blk(0, "nhgd"), blk(1, "nhgd"), blk(1, "hngd"), _tile(n, 256))
        ffnt = _peer_dense(ct, peer_u[l].astype(BF16), peer_v[l].T.astype(BF16), n_a, e1, r2, e2,
                           _tile(n, 1024), 8)
        h = _final(h1, ffnt, p[l].reshape(n, -1), row(ple_norm_g[l]), ple_w_gate[l].astype(BF16),
                   ple_w_proj[l].astype(BF16), row(final_norm_g), _tile(seq, 256), l == w_in.shape[0] - 1)
    return h.reshape(bsz, seq, d)
```

```python
import functools

import numpy as np
import jax
import jax.numpy as jnp
from jax import lax
from jax.experimental import pallas as pl
from jax.experimental.pallas import tpu as pltpu

F32 = jnp.float32
BF16 = jnp.bfloat16

RWKV_HEADS = 8
RWKV_HEAD_DIM = 64
RWKV_WIDTH = RWKV_HEADS * RWKV_HEAD_DIM
DECAY_LORA = 64
AAA_LORA = 64
GATE_LORA = 128
RWKV_COLS = 3 * RWKV_WIDTH + DECAY_LORA + AAA_LORA + GATE_LORA
GN_EPS = 64e-5
CONV_WIDTH = 512
CONV_K = 31
CONV_HALO = 32
LN_EPS = 1e-5
RMS_EPS = 1e-6
PEER_HEADS = 8
N_KEYS = 128
PEER_HALF = 64
PEER_TOPK = 16

VMEM_LIMIT = 56 * 1024 * 1024
LANES = 128
SUBLANES = 8


def _cparams(sem):
    return pltpu.CompilerParams(dimension_semantics=sem, vmem_limit_bytes=VMEM_LIMIT)


def _dot(a, b):
    return jnp.dot(a, b, preferred_element_type=F32)


def _dot_nt(a, b):
    return lax.dot_general(a, b, (((1,), (1,)), ((), ())), preferred_element_type=F32)


def _dot_split(x, m_bf16):
    hi = x.astype(BF16)
    lo = (x - hi.astype(F32)).astype(BF16)
    return _dot(hi, m_bf16) + _dot(lo, m_bf16)


def _rms(x, g):
    return x * lax.rsqrt(jnp.mean(x * x, axis=-1, keepdims=True) + RMS_EPS) * g


def _in_proj_kernel(x_ref, g_ref, w_ref, zr_ref, hg_ref, sg_ref):
    a = _rms(x_ref[...], g_ref[...]).astype(BF16)
    c0, c1, c2 = RWKV_COLS, RWKV_COLS + CONV_WIDTH, RWKV_COLS + 2 * CONV_WIDTH
    zr_ref[...] = _dot(a, w_ref[:, :c0])
    u = _dot(a, w_ref[:, c0:c1])
    gate = _dot(a, w_ref[:, c1:c2])
    hg_ref[...] = u * jax.nn.sigmoid(gate)
    sg_ref[...] = jax.nn.sigmoid(_dot(a, w_ref[:, c2:])).astype(BF16)


def _in_proj(x, g, w_in, tm):
    n, d = x.shape
    cols = w_in.shape[1]
    gate_cols = cols - RWKV_COLS - 2 * CONV_WIDTH
    row = lambda w: pl.BlockSpec((tm, w), lambda i: (i, 0))
    full = lambda a: pl.BlockSpec(a.shape, lambda i: (0,) * a.ndim)
    return pl.pallas_call(
        _in_proj_kernel,
        grid=(n // tm,),
        in_specs=[row(d), full(g), full(w_in)],
        out_specs=[row(RWKV_COLS), row(CONV_WIDTH), row(gate_cols)],
        out_shape=[jax.ShapeDtypeStruct((n, RWKV_COLS), F32),
                   jax.ShapeDtypeStruct((n, CONV_WIDTH), F32),
                   jax.ShapeDtypeStruct((n, gate_cols), BF16)],
        compiler_params=_cparams(("arbitrary",)),
        name="in_proj",
    )(x, g, w_in)


def _rwkv_prep_kernel(tiles_per_seq, z_ref, mu_ref, w0_ref, wup_ref, a0_ref, aup_ref, gup_ref,
                      kk_ref, ka_ref, rk_ref, seg_ref,
                      r_out, w_out, k_out, v_out, nkk_out, kka_out, g_out, bonus_out, last_ref):
    i = pl.program_id(0)

    @pl.when(i % tiles_per_seq == 0)
    def _():
        last_ref[...] = jnp.zeros_like(last_ref)

    z = z_ref[...]
    tm = z.shape[0]
    rows = lax.broadcasted_iota(jnp.int32, z.shape, 0)
    zs = jnp.where(rows == 0, last_ref[...], pltpu.roll(z, 1, axis=0))
    last_ref[...] = z[tm - 1:tm, :]
    z = z + (zs - z) * mu_ref[...]

    s0 = RWKV_WIDTH
    r = z[:, :s0]
    k = z[:, s0:2 * s0]
    v = z[:, 2 * s0:3 * s0]
    o = 3 * s0
    wl = z[:, o:o + DECAY_LORA]
    al = z[:, o + DECAY_LORA:o + DECAY_LORA + AAA_LORA]
    gl = z[:, o + DECAY_LORA + AAA_LORA:]

    w_log = -jax.nn.softplus(-(w0_ref[...] + _dot_split(jnp.tanh(wl), wup_ref[...]))) - 0.5
    decay = jnp.exp(-jnp.exp(w_log))
    a = jax.nn.sigmoid(a0_ref[...] + _dot_split(al, aup_ref[...]))
    g = _dot_split(jax.nn.sigmoid(gl), gup_ref[...])

    seg = seg_ref[...]
    kk = k * kk_ref[...]
    kk = kk * lax.rsqrt(jnp.maximum(_dot_split(kk * kk, seg), 1e-24))
    k2 = k * (1.0 + (a - 1.0) * ka_ref[...])
    bonus = _dot_split(r * k2 * rk_ref[...], seg) * v

    r_out[...] = r
    w_out[...] = decay
    k_out[...] = k2
    v_out[...] = v
    nkk_out[...] = -kk
    kka_out[...] = kk * a
    g_out[...] = g
    bonus_out[...] = bonus


def _rwkv_prep(zr, mu, w0, w_up, a0, a_up, g_up, k_k, k_a, r_k, seg, tm, seq):
    n = zr.shape[0]
    row = lambda w: pl.BlockSpec((tm, w), lambda i: (i, 0))
    full = lambda a: pl.BlockSpec(a.shape, lambda i: (0,) * a.ndim)
    consts = (mu, w0, w_up, a0, a_up, g_up, k_k, k_a, r_k, seg)
    return pl.pallas_call(
        functools.partial(_rwkv_prep_kernel, seq // tm),
        grid=(n // tm,),
        in_specs=[row(RWKV_COLS)] + [full(c) for c in consts],
        out_specs=[row(RWKV_WIDTH)] * 8,
        out_shape=[jax.ShapeDtypeStruct((n, RWKV_WIDTH), F32)] * 8,
        scratch_shapes=[pltpu.VMEM((1, RWKV_COLS), F32)],
        compiler_params=_cparams(("arbitrary",)),
        name="rwkv_prep",
    )(zr, *consts)


def _rwkv_scan_kernel(r_ref, w_ref, k_ref, v_ref, kka_ref, nkk_ref, y_ref, s_ref, sa_ref, yt_ref, rows_ref, next_ref,
                      pairs_ref):
    @pl.when(pl.program_id(0) == 0)
    def _():
        s_ref[...] = jnp.zeros_like(s_ref)
        yt_ref[...] = jnp.zeros_like(yt_ref)

    bsz, steps, width = r_ref.shape
    nh, hd = RWKV_HEADS, RWKV_HEAD_DIM
    chains = bsz * nh

    def by_pairs(ref, t):
        x = ref[:, t, :]
        return jnp.concatenate([x[:, j * LANES:(j + 1) * LANES] for j in range(width // LANES)], axis=0)

    def chains_on_lanes(pairs):
        xt = pairs.T
        return jnp.concatenate([xt[:hd, :], xt[hd:, :]], axis=1)

    to_lanes = lambda ref, t: chains_on_lanes(by_pairs(ref, t))

    w_i, k_i, kka_i, r_i, nkk_i, v_i = range(6)

    def gather(t):
        t = jnp.minimum(t, steps - 1)
        for i, ref in ((w_i, w_ref), (k_i, k_ref), (kka_i, kka_ref), (r_i, r_ref), (v_i, v_ref)):
            pairs_ref[i] = by_pairs(ref, t)
        pairs_ref[nkk_i] = by_pairs(nkk_ref, jnp.minimum(t + 1, steps - 1))

    def flip():
        for i in range(6):
            next_ref[i] = chains_on_lanes(pairs_ref[i])

    def emit(t):
        y = yt_ref[...]
        yt = jnp.concatenate([y[:, :chains // 2], y[:, chains // 2:]], axis=0).T
        y_ref[:, t, :] = jnp.concatenate([yt[j * bsz:(j + 1) * bsz, :] for j in range(width // LANES)], axis=1)

    rows_ref[nkk_i] = to_lanes(nkk_ref, 0)
    part = [jnp.zeros((hd, chains), F32) for _ in range(4)]
    for kx in range(hd):
        part[kx % 4] = part[kx % 4] + s_ref[kx] * rows_ref[nkk_i, pl.ds(kx, 1), :]
    sa_ref[...] = (part[0] + part[1]) + (part[2] + part[3])
    gather(0)
    flip()
    rows_ref[...] = next_ref[...]
    gather(1)

    def step(t, carry):
        flip()
        row = lambda i, kx: rows_ref[i, pl.ds(kx, 1), :]
        v_t, sa_old = rows_ref[v_i], sa_ref[...]
        y = [jnp.zeros_like(v_t), jnp.zeros_like(v_t)]
        sa = [jnp.zeros_like(v_t), jnp.zeros_like(v_t)]
        for kx in range(hd):
            s_k = s_ref[kx] * row(w_i, kx) + sa_old * row(kka_i, kx) + v_t * row(k_i, kx)
            s_ref[kx] = s_k
            y[kx % 2] = y[kx % 2] + s_k * row(r_i, kx)
            sa[kx % 2] = sa[kx % 2] + s_k * row(nkk_i, kx)
        sa_ref[...] = sa[0] + sa[1]
        emit(jnp.maximum(t - 1, 0))
        yt_ref[...] = y[0] + y[1]
        gather(t + 2)
        rows_ref[...] = next_ref[...]
        return carry

    lax.fori_loop(0, steps, step, 0)
    emit(steps - 1)


def _rwkv_scan(r, w, k, v, kka, nkk, steps):
    bsz, t, width = r.shape
    nh, hd = RWKV_HEADS, RWKV_HEAD_DIM
    blk = pl.BlockSpec((bsz, steps, width), lambda i: (0, i, 0))
    return pl.pallas_call(
        _rwkv_scan_kernel,
        grid=(t // steps,),
        in_specs=[blk] * 6,
        out_specs=blk,
        out_shape=jax.ShapeDtypeStruct((bsz, t, width), F32),
        scratch_shapes=[pltpu.VMEM((hd, hd, bsz * nh), F32), pltpu.VMEM((hd, bsz * nh), F32),
                        pltpu.VMEM((hd, bsz * nh), F32), pltpu.VMEM((6, hd, bsz * nh), F32),
                        pltpu.VMEM((6, hd, bsz * nh), F32), pltpu.VMEM((6, bsz * width // LANES, LANES), F32)],
        compiler_params=_cparams(("arbitrary",)),
        name="rwkv_scan",
    )(r, w, k, v, kka, nkk)


def _mix_out_kernel(tiles_per_seq, x_ref, y_ref, g_ref, bonus_ref, hg_ref, sg_ref, seg_ref,
                    lnxg_ref, lnxb_ref, wor_ref, dww_ref, dwb_ref, clng_ref, clnb_ref, woc_ref,
                    wout_ref, nfg_ref, wq_ref, h1_out, ct_out, q_out, buf_ref):
    i = pl.program_id(0)
    tm = x_ref.shape[0]

    @pl.when(i % tiles_per_seq == 0)
    def _():
        buf_ref[:CONV_HALO, :] = jnp.zeros((CONV_HALO, CONV_WIDTH), F32)
        buf_ref[CONV_HALO + tm:, :] = jnp.zeros((SUBLANES, CONV_WIDTH), F32)

    seg = seg_ref[...]
    y = y_ref[...]
    inv_n = 1.0 / RWKV_HEAD_DIM
    mean = _dot_split(y, seg) * inv_n
    yc = y - mean
    var = _dot_split(yc * yc, seg) * inv_n
    yn = yc * lax.rsqrt(var + GN_EPS) * lnxg_ref[...] + lnxb_ref[...]
    o_r = _dot(((yn + bonus_ref[...]) * g_ref[...]).astype(BF16), wor_ref[...])

    buf_ref[CONV_HALO:CONV_HALO + tm, :] = hg_ref[...]
    acc = jnp.zeros((tm, CONV_WIDTH), F32) + dwb_ref[...]
    for phase in range(SUBLANES):
        part = None
        for j in range(CONV_K):
            off = CONV_HALO - (CONV_K - 1) + j
            if off % SUBLANES == phase:
                base = off - phase
                term = buf_ref[base:base + tm + SUBLANES, :] * dww_ref[j:j + 1, :]
                part = term if part is None else part + term
        acc = acc + part[phase:phase + tm, :]
    buf_ref[:CONV_HALO, :] = buf_ref[tm:tm + CONV_HALO, :]
    mu = jnp.mean(acc, axis=-1, keepdims=True)
    ac = acc - mu
    cv = jnp.mean(ac * ac, axis=-1, keepdims=True)
    hf = ac * lax.rsqrt(cv + LN_EPS) * clng_ref[...] + clnb_ref[...]
    o_c = _dot((hf * jax.nn.sigmoid(hf)).astype(BF16), woc_ref[...])

    d = x_ref.shape[1]
    sg = sg_ref[...].astype(F32)
    mix = sg[:, :d] * o_r + sg[:, d:] * o_c
    h1 = x_ref[...] + _dot(mix.astype(BF16), wout_ref[...])
    h1_out[...] = h1
    c = _rms(h1, nfg_ref[...])
    ct_out[...] = c.T.astype(BF16)
    q_out[...] = _dot(c.astype(BF16), wq_ref[...])


def _mix_out(x, y, g, bonus, hg, sg, seg, consts, tm, seq):
    n, d = x.shape
    row = lambda w: pl.BlockSpec((tm, w), lambda i: (i, 0))
    full = lambda a: pl.BlockSpec(a.shape, lambda i: (0,) * a.ndim)
    return pl.pallas_call(
        functools.partial(_mix_out_kernel, seq // tm),
        grid=(n // tm,),
        in_specs=[row(d), row(RWKV_WIDTH), row(RWKV_WIDTH), row(RWKV_WIDTH), row(CONV_WIDTH), row(2 * d),
                  full(seg)] + [full(c) for c in consts],
        out_specs=[row(d), pl.BlockSpec((d, tm), lambda i: (0, i)), row(d)],
        out_shape=[jax.ShapeDtypeStruct((n, d), F32), jax.ShapeDtypeStruct((d, n), BF16),
                   jax.ShapeDtypeStruct((n, d), F32)],
        scratch_shapes=[pltpu.VMEM((tm + CONV_HALO + SUBLANES, CONV_WIDTH), F32)],
        compiler_params=_cparams(("arbitrary",)),
        name="mix_out",
    )(x, y, g, bonus, hg, sg, seg, *consts)


def _sort16_pairs():
    n, pairs, p = PEER_TOPK, [], 1
    while p < n:
        k = p
        while k >= 1:
            for j in range(k % p, n - k, 2 * k):
                for i in range(min(k, n - j - k)):
                    if (i + j) // (2 * p) == (i + j + k) // (2 * p):
                        pairs.append((i + j, i + j + k))
            k //= 2
        p *= 2
    return pairs


_SORT16 = _sort16_pairs()
_BITONIC16 = [(i, i + dd) for dd in (8, 4, 2, 1) for i in range(PEER_TOPK) if not i & dd]
_CELLS = [(i, j) for i in range(PEER_TOPK) for j in range(PEER_TOPK) if (i + 1) * (j + 1) <= PEER_TOPK]


def _exchange(vals, pairs):
    for a, b in pairs:
        hi, lo = jnp.maximum(vals[a], vals[b]), jnp.minimum(vals[a], vals[b])
        vals[a], vals[b] = hi, lo
    return vals


def _top16_sorted(slabs):
    groups = []
    for s in range(0, len(slabs), PEER_TOPK):
        groups.append(_exchange(list(slabs[s:s + PEER_TOPK]), _SORT16))
    while len(groups) > 1:
        merged = []
        for a, b in zip(groups[::2], groups[1::2]):
            c = [jnp.maximum(a[i], b[PEER_TOPK - 1 - i]) for i in range(PEER_TOPK)]
            merged.append(_exchange(c, _BITONIC16))
        groups = merged
    return groups[0]


def _peer_route_kernel(q_ref, ska0_ref, ska1_ref, skb1_ref, n_out, e1_out, r2_out, e2_out, s1_ref, s2_ref, v2_ref):
    qh = q_ref[...].astype(BF16)
    half = qh.shape[1] // 2
    s1_ref[...] = _dot_nt(ska0_ref[...], qh[:, :half])
    s2_ref[...] = _dot_nt(ska1_ref[...], qh[:, half:])
    nh = PEER_HEADS
    slab = lambda ref, n: ref[n * nh:(n + 1) * nh, :]

    v1 = _top16_sorted([slab(s1_ref, n) for n in range(N_KEYS)])
    v2 = _top16_sorted([slab(s2_ref, n) for n in range(N_KEYS)])

    cand = {c: v1[c[0]] + v2[c[1]] for c in _CELLS}
    cnt = {c: jnp.full_like(v1[0], float((c[0] + 1) * (c[1] + 1) - 1)) for c in _CELLS}
    for c1 in _CELLS:
        for c2 in _CELLS:
            if c1[0] < c2[0] and c1[1] > c2[1]:
                first = cand[c1] >= cand[c2]
                cnt[c2] = cnt[c2] + jnp.where(first, 1.0, 0.0)
                cnt[c1] = cnt[c1] + jnp.where(first, 0.0, 1.0)
    top = cand[(0, 0)]
    z = jnp.zeros_like(top)
    taken = [jnp.zeros_like(top) for _ in range(PEER_TOPK)]
    for c in _CELLS:
        sel = cnt[c] < float(PEER_TOPK)
        z = z + jnp.where(sel, jnp.exp(cand[c] - top), 0.0)
        taken[c[0]] = taken[c[0]] + jnp.where(sel, 1.0, 0.0)
    inv_z = 1.0 / z

    for n in range(N_KEYS):
        s = slab(s1_ref, n)
        cnt_a = jnp.zeros_like(s)
        for i in reversed(range(PEER_TOPK)):
            cnt_a = jnp.where(s == v1[i], taken[i], cnt_a)
        n_out[n * nh:(n + 1) * nh, :] = cnt_a
        e1_out[n * nh:(n + 1) * nh, :] = jnp.exp(s - v1[0]) * inv_z

    for j in range(PEER_TOPK):
        v2_ref[j] = v2[j]
    s2b = _dot_nt(skb1_ref[...], qh[:, half:])
    for h in range(nh):
        s = s2b[h * N_KEYS:(h + 1) * N_KEYS, :]
        rank = jnp.zeros_like(s)
        for j in range(PEER_TOPK):
            rank = jnp.where(v2_ref[j, h:h + 1, :] > s, float(j + 1), rank)
        r2_out[h * N_KEYS:(h + 1) * N_KEYS, :] = rank.astype(BF16)
        e2_out[h * N_KEYS:(h + 1) * N_KEYS, :] = jnp.exp(s - v2_ref[0, h:h + 1, :]).astype(BF16)


def _peer_route(q, ska0, ska1, skb1, tn):
    n, d = q.shape
    rows = PEER_HEADS * N_KEYS
    full = lambda a: pl.BlockSpec(a.shape, lambda i: (0,) * a.ndim)
    col = pl.BlockSpec((rows, tn), lambda i: (0, i))
    return pl.pallas_call(
        _peer_route_kernel,
        grid=(n // tn,),
        in_specs=[pl.BlockSpec((tn, d), lambda i: (i, 0)), full(ska0), full(ska1), full(skb1)],
        out_specs=[col] * 4,
        out_shape=[jax.ShapeDtypeStruct((rows, n), F32), jax.ShapeDtypeStruct((rows, n), F32),
                   jax.ShapeDtypeStruct((rows, n), BF16), jax.ShapeDtypeStruct((rows, n), BF16)],
        scratch_shapes=[pltpu.VMEM((rows, tn), F32), pltpu.VMEM((rows, tn), F32),
                        pltpu.VMEM((PEER_TOPK, PEER_HEADS, tn), F32)],
        compiler_params=_cparams(("arbitrary",)),
        name="peer_route",
    )(q, ska0, ska1, skb1)


def _peer_dense_kernel(n_eb, ct_ref, u_ref, vt_ref, n_ref, e1_ref, r2_ref, e2_ref, out_ref,
                       ht0_ref, ht1_ref, cf0_ref, cf1_ref, nb_ref, eb_ref):
    s = pl.program_id(0)

    @pl.when(s == 0)
    def _():
        for ref in (ht0_ref, ht1_ref, cf0_ref, cf1_ref):
            ref[...] = jnp.zeros_like(ref)

    @pl.when((s <= 2) | ((s - 2) % n_eb == 0))
    def _():
        out_ref[...] = jnp.zeros_like(out_ref)

    nh = PEER_HEADS
    tn = ct_ref.shape[1]
    rg = 16
    a_blk = u_ref.shape[0] // N_KEYS
    zero = jnp.zeros((rg, tn), BF16)
    g_half = N_KEYS // rg // 2

    def step(ht_new, ht_old, cf_old, cf_new):
        n_chunk = a_blk // 2
        e_rows, d_rows = u_ref.shape[0] // n_chunk, out_ref.shape[0] // n_chunk
        for ch in range(n_chunk):
            rows = slice(ch * e_rows, (ch + 1) * e_rows)
            ht_new[rows, :] = _dot(u_ref[rows, :], ct_ref[...])
            for r in range(2 * ch * nh, (2 * ch + 2) * nh):
                for cols in [slice(lt * LANES, (lt + 1) * LANES) for lt in range(tn // LANES)]:
                    for src, dst in ((n_ref, nb_ref), (e1_ref, eb_ref)):
                        dst[r * rg:(r + 1) * rg, cols] = jnp.broadcast_to(src[r:r + 1, cols], (rg, LANES)).astype(BF16)
            for al, g0 in [(a, g) for a in range(2 * ch, 2 * ch + 2) for g in (0, g_half)]:
                gate = [zero] * g_half
                for h in range(nh):
                    r = al * nh + h
                    n_a, e1_a = nb_ref[r * rg:(r + 1) * rg, :], eb_ref[r * rg:(r + 1) * rg, :]
                    for g in range(g_half):
                        rows = slice(h * N_KEYS + (g0 + g) * rg, h * N_KEYS + (g0 + g + 1) * rg)
                        gate[g] = gate[g] + jnp.where(r2_ref[rows, :] < n_a, e2_ref[rows, :], zero) * e1_a
                for g in range(g_half):
                    rows = slice(al * N_KEYS + (g0 + g) * rg, al * N_KEYS + (g0 + g + 1) * rg)
                    ht = ht_old[rows, :].astype(BF16)
                    act = ht * (0.5 * (1.0 + lax.erf(ht * (2.0 ** -0.5))))
                    cf_old[rows, :] = act * gate[g]
            rows = slice(ch * d_rows, (ch + 1) * d_rows)
            out_ref[rows, :] += _dot(vt_ref[rows, :], cf_new[...])

    @pl.when(s % 2 == 0)
    def _():
        step(ht0_ref, ht1_ref, cf1_ref, cf0_ref)

    @pl.when(s % 2 == 1)
    def _():
        step(ht1_ref, ht0_ref, cf0_ref, cf1_ref)


def _peer_dense(ct, u_bf, vt_bf, n_a, e1, r2, e2, tn, a_blk):
    d, n = ct.shape
    e_blk = a_blk * N_KEYS
    n_eb = u_bf.shape[0] // e_blk
    last = (n // tn) * n_eb - 1
    rows = PEER_HEADS * N_KEYS
    pair = lambda s, lag: jnp.clip(s - lag, 0, last)
    tok = lambda s, lag: pair(s, lag) // n_eb
    blk = lambda s, lag: pair(s, lag) % n_eb
    return pl.pallas_call(
        functools.partial(_peer_dense_kernel, n_eb),
        grid=(last + 3,),
        in_specs=[pl.BlockSpec((d, tn), lambda s: (0, tok(s, 0))),
                  pl.BlockSpec((e_blk, d), lambda s: (blk(s, 0), 0)),
                  pl.BlockSpec((d, e_blk), lambda s: (0, blk(s, 2))),
                  pl.BlockSpec((a_blk * PEER_HEADS, tn), lambda s: (blk(s, 1), tok(s, 1))),
                  pl.BlockSpec((a_blk * PEER_HEADS, tn), lambda s: (blk(s, 1), tok(s, 1))),
                  pl.BlockSpec((rows, tn), lambda s: (0, tok(s, 1))),
                  pl.BlockSpec((rows, tn), lambda s: (0, tok(s, 1)))],
        out_specs=pl.BlockSpec((d, tn), lambda s: (0, tok(s, 2))),
        out_shape=jax.ShapeDtypeStruct((d, n), F32),
        scratch_shapes=([pltpu.VMEM((e_blk, tn), F32)] * 2 + [pltpu.VMEM((e_blk, tn), BF16)] * 2
                        + [pltpu.VMEM((a_blk * PEER_HEADS * 16, tn), BF16)] * 2),
        compiler_params=_cparams(("arbitrary",)),
        name="peer_dense",
    )(ct, u_bf, vt_bf, n_a, e1, r2, e2)


def _final_kernel(last_layer, h1_ref, ffnt_ref, p_ref, png_ref, wg_ref, wp_ref, fng_ref, out_ref):
    h2 = h1_ref[...] + ffnt_ref[...].T
    gate = jax.nn.sigmoid(_dot(_rms(h2, png_ref[...]).astype(BF16), wg_ref[...]))
    h3 = h2 + gate * _dot(p_ref[...].astype(BF16), wp_ref[...])
    out_ref[...] = _rms(h3, fng_ref[...]) if last_layer else h3


def _final(h1, ffnt, p, png, wg, wp, fng, tm, last_layer):
    n, d = h1.shape
    full = lambda a: pl.BlockSpec(a.shape, lambda i: (0,) * a.ndim)
    return pl.pallas_call(
        functools.partial(_final_kernel, last_layer),
        grid=(n // tm,),
        in_specs=[pl.BlockSpec((tm, d), lambda i: (i, 0)), pl.BlockSpec((d, tm), lambda i: (0, i)),
                  pl.BlockSpec((tm, p.shape[1]), lambda i: (i, 0)), full(png), full(wg), full(wp), full(fng)],
        out_specs=pl.BlockSpec((tm, d), lambda i: (i, 0)),
        out_shape=jax.ShapeDtypeStruct((n, d), F32),
        compiler_params=_cparams(("arbitrary",)),
        name="final",
    )(h1, ffnt, p, png, wg, wp, fng)


def _tile(n, want):
    t = min(n, want)
    assert n % t == 0, (n, t)
    return t


def kernel(x, p, norm_mix_g, w_in, rwkv_mu, rwkv_w0, rwkv_w_up, rwkv_a0, rwkv_a_up, rwkv_g_up, rwkv_k_k, rwkv_k_a, rwkv_r_k, rwkv_lnx_g, rwkv_lnx_b, rwkv_w_o, conv_dw_w, conv_dw_b, conv_ln_g, conv_ln_b, conv_w_o, w_out, norm_ffn_g, peer_w_q, peer_sub_keys, peer_u, peer_v, ple_w_proj, ple_norm_g, ple_w_gate, final_norm_g):
    bsz, seq, d = x.shape
    n = bsz * seq
    nh, hd = RWKV_HEADS, RWKV_HEAD_DIM
    row = lambda a: a.reshape(1, -1)
    head_of = np.arange(RWKV_WIDTH) // hd
    seg = jnp.asarray(head_of[:, None] == head_of[None, :], BF16)
    h = x.reshape(n, d)

    for l in range(w_in.shape[0]):
        zr, hg, sg = _in_proj(h, row(norm_mix_g[l]), w_in[l].astype(BF16), _tile(seq, 512))
        r, w, k, v, nkk, kka, g, bonus = _rwkv_prep(
            zr, row(rwkv_mu[l]), row(rwkv_w0[l]), rwkv_w_up[l].astype(BF16), row(rwkv_a0[l]),
            rwkv_a_up[l].astype(BF16), rwkv_g_up[l].astype(BF16),
            row(rwkv_k_k[l]), row(rwkv_k_a[l]), row(rwkv_r_k[l]), seg, _tile(seq, 512), seq)

        seqs = lambda a: a.reshape(bsz, seq, RWKV_WIDTH)
        y = _rwkv_scan(seqs(r), seqs(w), seqs(k), seqs(v), seqs(kka), seqs(nkk), _tile(seq, 32))
        y = y.reshape(n, RWKV_WIDTH)

        consts = (row(rwkv_lnx_g[l]), row(rwkv_lnx_b[l]), rwkv_w_o[l].astype(BF16), conv_dw_w[l],
                  row(conv_dw_b[l]), row(conv_ln_g[l]), row(conv_ln_b[l]), conv_w_o[l].astype(BF16),
                  w_out[l].astype(BF16), row(norm_ffn_g[l]),
                  peer_w_q[l].reshape(d, PEER_HEADS, 2, PEER_HALF).transpose(0, 2, 1, 3).reshape(d, -1).astype(BF16))
        h1, ct, q = _mix_out(h, y, g, bonus, hg, sg, seg, consts, _tile(seq, 256), seq)

        sk = peer_sub_keys[l]
        eye = jnp.eye(PEER_HEADS, dtype=sk.dtype)
        blk = lambda half, order: jnp.einsum("hnd,hg->" + order, sk[:, half], eye).reshape(
            PEER_HEADS * N_KEYS, PEER_HEADS * PEER_HALF).astype(BF16)
        n_a, e1, r2, e2 = _peer_route(q, blk(0, "nhgd"), blk(1, "nhgd"), blk(1, "hngd"), _tile(n, 256))
        ffnt = _peer_dense(ct, peer_u[l].astype(BF16), peer_v[l].T.astype(BF16), n_a, e1, r2, e2,
                           _tile(n, 1024), 8)
        h = _final(h1, ffnt, p[l].reshape(n, -1), row(ple_norm_g[l]), ple_w_gate[l].astype(BF16),
                   ple_w_proj[l].astype(BF16), row(final_norm_g), _tile(seq, 256), l == w_in.shape[0] - 1)
    return h.reshape(bsz, seq, d)
```

```python
import functools

import numpy as np
import jax
import jax.numpy as jnp
from jax import lax
from jax.experimental import pallas as pl
from jax.experimental.pallas import tpu as pltpu

F32 = jnp.float32
BF16 = jnp.bfloat16

RWKV_HEADS = 8
RWKV_HEAD_DIM = 64
RWKV_WIDTH = RWKV_HEADS * RWKV_HEAD_DIM
DECAY_LORA = 64
AAA_LORA = 64
GATE_LORA = 128
RWKV_COLS = 3 * RWKV_WIDTH + DECAY_LORA + AAA_LORA + GATE_LORA
GN_EPS = 64e-5
CONV_WIDTH = 512
CONV_K = 31
CONV_HALO = 32
LN_EPS = 1e-5
RMS_EPS = 1e-6
PEER_HEADS = 8
N_KEYS = 128
PEER_HALF = 64
PEER_TOPK = 16

VMEM_LIMIT = 56 * 1024 * 1024
LANES = 128
SUBLANES = 8


def _cparams(sem):
    return pltpu.CompilerParams(dimension_semantics=sem, vmem_limit_bytes=VMEM_LIMIT)


def _dot(a, b):
    return jnp.dot(a, b, preferred_element_type=F32)


def _dot_nt(a, b):
    return lax.dot_general(a, b, (((1,), (1,)), ((), ())), preferred_element_type=F32)


def _dot_split(x, m_bf16):
    hi = x.astype(BF16)
    lo = (x - hi.astype(F32)).astype(BF16)
    return _dot(hi, m_bf16) + _dot(lo, m_bf16)


def _rms(x, g):
    return x * lax.rsqrt(jnp.mean(x * x, axis=-1, keepdims=True) + RMS_EPS) * g


def _in_proj_kernel(x_ref, g_ref, w_ref, zr_ref, hg_ref, sg_ref):
    a = _rms(x_ref[...], g_ref[...]).astype(BF16)
    c0, c1, c2 = RWKV_COLS, RWKV_COLS + CONV_WIDTH, RWKV_COLS + 2 * CONV_WIDTH
    zr_ref[...] = _dot(a, w_ref[:, :c0])
    u = _dot(a, w_ref[:, c0:c1])
    gate = _dot(a, w_ref[:, c1:c2])
    hg_ref[...] = u * jax.nn.sigmoid(gate)
    sg_ref[...] = jax.nn.sigmoid(_dot(a, w_ref[:, c2:])).astype(BF16)


def _in_proj(x, g, w_in, tm):
    n, d = x.shape
    cols = w_in.shape[1]
    gate_cols = cols - RWKV_COLS - 2 * CONV_WIDTH
    row = lambda w: pl.BlockSpec((tm, w), lambda i: (i, 0))
    full = lambda a: pl.BlockSpec(a.shape, lambda i: (0,) * a.ndim)
    return pl.pallas_call(
        _in_proj_kernel,
        grid=(n // tm,),
        in_specs=[row(d), full(g), full(w_in)],
        out_specs=[row(RWKV_COLS), row(CONV_WIDTH), row(gate_cols)],
        out_shape=[jax.ShapeDtypeStruct((n, RWKV_COLS), F32),
                   jax.ShapeDtypeStruct((n, CONV_WIDTH), F32),
                   jax.ShapeDtypeStruct((n, gate_cols), BF16)],
        compiler_params=_cparams(("arbitrary",)),
        name="in_proj",
    )(x, g, w_in)


def _rwkv_prep_kernel(tiles_per_seq, z_ref, mu_ref, w0_ref, wup_ref, a0_ref, aup_ref, gup_ref,
                      kk_ref, ka_ref, rk_ref, seg_ref,
                      r_out, w_out, k_out, v_out, nkk_out, kka_out, g_out, bonus_out, last_ref):
    i = pl.program_id(0)

    @pl.when(i % tiles_per_seq == 0)
    def _():
        last_ref[...] = jnp.zeros_like(last_ref)

    z = z_ref[...]
    tm = z.shape[0]
    rows = lax.broadcasted_iota(jnp.int32, z.shape, 0)
    zs = jnp.where(rows == 0, last_ref[...], pltpu.roll(z, 1, axis=0))
    last_ref[...] = z[tm - 1:tm, :]
    z = z + (zs - z) * mu_ref[...]

    s0 = RWKV_WIDTH
    r = z[:, :s0]
    k = z[:, s0:2 * s0]
    v = z[:, 2 * s0:3 * s0]
    o = 3 * s0
    wl = z[:, o:o + DECAY_LORA]
    al = z[:, o + DECAY_LORA:o + DECAY_LORA + AAA_LORA]
    gl = z[:, o + DECAY_LORA + AAA_LORA:]

    w_log = -jax.nn.softplus(-(w0_ref[...] + _dot_split(jnp.tanh(wl), wup_ref[...]))) - 0.5
    decay = jnp.exp(-jnp.exp(w_log))
    a = jax.nn.sigmoid(a0_ref[...] + _dot_split(al, aup_ref[...]))
    g = _dot_split(jax.nn.sigmoid(gl), gup_ref[...])

    seg = seg_ref[...]
    kk = k * kk_ref[...]
    kk = kk * lax.rsqrt(jnp.maximum(_dot_split(kk * kk, seg), 1e-24))
    k2 = k * (1.0 + (a - 1.0) * ka_ref[...])
    bonus = _dot_split(r * k2 * rk_ref[...], seg) * v

    r_out[...] = r
    w_out[...] = decay
    k_out[...] = k2
    v_out[...] = v
    nkk_out[...] = -kk
    kka_out[...] = kk * a
    g_out[...] = g
    bonus_out[...] = bonus


def _rwkv_prep(zr, mu, w0, w_up, a0, a_up, g_up, k_k, k_a, r_k, seg, tm, seq):
    n = zr.shape[0]
    row = lambda w: pl.BlockSpec((tm, w), lambda i: (i, 0))
    full = lambda a: pl.BlockSpec(a.shape, lambda i: (0,) * a.ndim)
    consts = (mu, w0, w_up, a0, a_up, g_up, k_k, k_a, r_k, seg)
    return pl.pallas_call(
        functools.partial(_rwkv_prep_kernel, seq // tm),
        grid=(n // tm,),
        in_specs=[row(RWKV_COLS)] + [full(c) for c in consts],
        out_specs=[row(RWKV_WIDTH)] * 8,
        out_shape=[jax.ShapeDtypeStruct((n, RWKV_WIDTH), F32)] * 8,
        scratch_shapes=[pltpu.VMEM((1, RWKV_COLS), F32)],
        compiler_params=_cparams(("arbitrary",)),
        name="rwkv_prep",
    )(zr, *consts)


def _rwkv_scan_kernel(r_ref, w_ref, k_ref, v_ref, kka_ref, nkk_ref, y_ref, s_ref, sa_ref, yt_ref, rows_ref, next_ref,
                      pairs_ref):
    @pl.when(pl.program_id(0) == 0)
    def _():
        s_ref[...] = jnp.zeros_like(s_ref)
        yt_ref[...] = jnp.zeros_like(yt_ref)

    bsz, steps, width = r_ref.shape
    nh, hd = RWKV_HEADS, RWKV_HEAD_DIM
    chains = bsz * nh

    def by_pairs(ref, t):
        x = ref[:, t, :]
        return jnp.concatenate([x[:, j * LANES:(j + 1) * LANES] for j in range(width // LANES)], axis=0)

    def chains_on_lanes(pairs):
        xt = pairs.T
        return jnp.concatenate([xt[:hd, :], xt[hd:, :]], axis=1)

    to_lanes = lambda ref, t: chains_on_lanes(by_pairs(ref, t))

    w_i, k_i, kka_i, r_i, nkk_i, v_i = range(6)

    def gather(t):
        t = jnp.minimum(t, steps - 1)
        for i, ref in ((w_i, w_ref), (k_i, k_ref), (kka_i, kka_ref), (r_i, r_ref), (v_i, v_ref)):
            pairs_ref[i] = by_pairs(ref, t)
        pairs_ref[nkk_i] = by_pairs(nkk_ref, jnp.minimum(t + 1, steps - 1))

    def flip():
        for i in range(6):
            next_ref[i] = chains_on_lanes(pairs_ref[i])

    def emit(t):
        y = yt_ref[...]
        yt = jnp.concatenate([y[:, :chains // 2], y[:, chains // 2:]], axis=0).T
        y_ref[:, t, :] = jnp.concatenate([yt[j * bsz:(j + 1) * bsz, :] for j in range(width // LANES)], axis=1)

    rows_ref[nkk_i] = to_lanes(nkk_ref, 0)
    part = [jnp.zeros((hd, chains), F32) for _ in range(4)]
    for kx in range(hd):
        part[kx % 4] = part[kx % 4] + s_ref[kx] * rows_ref[nkk_i, pl.ds(kx, 1), :]
    sa_ref[...] = (part[0] + part[1]) + (part[2] + part[3])
    gather(0)
    flip()
    rows_ref[...] = next_ref[...]
    gather(1)

    def step(t, carry):
        flip()
        row = lambda i, kx: rows_ref[i, pl.ds(kx, 1), :]
        v_t, sa_old = rows_ref[v_i], sa_ref[...]
        y = [jnp.zeros_like(v_t), jnp.zeros_like(v_t)]
        sa = [jnp.zeros_like(v_t), jnp.zeros_like(v_t)]
        for kx in range(hd):
            s_k = s_ref[kx] * row(w_i, kx) + sa_old * row(kka_i, kx) + v_t * row(k_i, kx)
            s_ref[kx] = s_k
            y[kx % 2] = y[kx % 2] + s_k * row(r_i, kx)
            sa[kx % 2] = sa[kx % 2] + s_k * row(nkk_i, kx)
        sa_ref[...] = sa[0] + sa[1]
        emit(jnp.maximum(t - 1, 0))
        yt_ref[...] = y[0] + y[1]
        gather(t + 2)
        rows_ref[...] = next_ref[...]
        return carry

    lax.fori_loop(0, steps, step, 0)
    emit(steps - 1)


def _rwkv_scan(r, w, k, v, kka, nkk, steps):
    bsz, t, width = r.shape
    nh, hd = RWKV_HEADS, RWKV_HEAD_DIM
    blk = pl.BlockSpec((bsz, steps, width), lambda i: (0, i, 0))
    return pl.pallas_call(
        _rwkv_scan_kernel,
        grid=(t // steps,),
        in_specs=[blk] * 6,
        out_specs=blk,
        out_shape=jax.ShapeDtypeStruct((bsz, t, width), F32),
        scratch_shapes=[pltpu.VMEM((hd, hd, bsz * nh), F32), pltpu.VMEM((hd, bsz * nh), F32),
                        pltpu.VMEM((hd, bsz * nh), F32), pltpu.VMEM((6, hd, bsz * nh), F32),
                        pltpu.VMEM((6, hd, bsz * nh), F32), pltpu.VMEM((6, bsz * width // LANES, LANES), F32)],
        compiler_params=_cparams(("arbitrary",)),
        name="rwkv_scan",
    )(r, w, k, v, kka, nkk)


def _mix_out_kernel(tiles_per_seq, x_ref, y_ref, g_ref, bonus_ref, hg_ref, sg_ref, seg_ref,
                    lnxg_ref, lnxb_ref, wor_ref, dww_ref, dwb_ref, clng_ref, clnb_ref, woc_ref,
                    wout_ref, nfg_ref, wq_ref, h1_out, ct_out, q_out, buf_ref):
    i = pl.program_id(0)
    tm = x_ref.shape[0]

    @pl.when(i % tiles_per_seq == 0)
    def _():
        buf_ref[:CONV_HALO, :] = jnp.zeros((CONV_HALO, CONV_WIDTH), F32)
        buf_ref[CONV_HALO + tm:, :] = jnp.zeros((SUBLANES, CONV_WIDTH), F32)

    seg = seg_ref[...]
    y = y_ref[...]
    inv_n = 1.0 / RWKV_HEAD_DIM
    mean = _dot_split(y, seg) * inv_n
    yc = y - mean
    var = _dot_split(yc * yc, seg) * inv_n
    yn = yc * lax.rsqrt(var + GN_EPS) * lnxg_ref[...] + lnxb_ref[...]
    o_r = _dot(((yn + bonus_ref[...]) * g_ref[...]).astype(BF16), wor_ref[...])

    buf_ref[CONV_HALO:CONV_HALO + tm, :] = hg_ref[...]
    acc = jnp.zeros((tm, CONV_WIDTH), F32) + dwb_ref[...]
    for phase in range(SUBLANES):
        part = None
        for j in range(CONV_K):
            off = CONV_HALO - (CONV_K - 1) + j
            if off % SUBLANES == phase:
                base = off - phase
                term = buf_ref[base:base + tm + SUBLANES, :] * dww_ref[j:j + 1, :]
                part = term if part is None else part + term
        acc = acc + part[phase:phase + tm, :]
    buf_ref[:CONV_HALO, :] = buf_ref[tm:tm + CONV_HALO, :]
    mu = jnp.mean(acc, axis=-1, keepdims=True)
    ac = acc - mu
    cv = jnp.mean(ac * ac, axis=-1, keepdims=True)
    hf = ac * lax.rsqrt(cv + LN_EPS) * clng_ref[...] + clnb_ref[...]
    o_c = _dot((hf * jax.nn.sigmoid(hf)).astype(BF16), woc_ref[...])

    d = x_ref.shape[1]
    sg = sg_ref[...].astype(F32)
    mix = sg[:, :d] * o_r + sg[:, d:] * o_c
    h1 = x_ref[...] + _dot(mix.astype(BF16), wout_ref[...])
    h1_out[...] = h1
    c = _rms(h1, nfg_ref[...])
    ct_out[...] = c.T.astype(BF16)
    q_out[...] = _dot(c.astype(BF16), wq_ref[...])


def _mix_out(x, y, g, bonus, hg, sg, seg, consts, tm, seq):
    n, d = x.shape
    row = lambda w: pl.BlockSpec((tm, w), lambda i: (i, 0))
    full = lambda a: pl.BlockSpec(a.shape, lambda i: (0,) * a.ndim)
    return pl.pallas_call(
        functools.partial(_mix_out_kernel, seq // tm),
        grid=(n // tm,),
        in_specs=[row(d), row(RWKV_WIDTH), row(RWKV_WIDTH), row(RWKV_WIDTH), row(CONV_WIDTH), row(2 * d),
                  full(seg)] + [full(c) for c in consts],
        out_specs=[row(d), pl.BlockSpec((d, tm), lambda i: (0, i)), row(d)],
        out_shape=[jax.ShapeDtypeStruct((n, d), F32), jax.ShapeDtypeStruct((d, n), BF16),
                   jax.ShapeDtypeStruct((n, d), F32)],
        scratch_shapes=[pltpu.VMEM((tm + CONV_HALO + SUBLANES, CONV_WIDTH), F32)],
        compiler_params=_cparams(("arbitrary",)),
        name="mix_out",
    )(x, y, g, bonus, hg, sg, seg, *consts)


def _sort16_pairs():
    n, pairs, p = PEER_TOPK, [], 1
    while p < n:
        k = p
        while k >= 1:
            for j in range(k % p, n - k, 2 * k):
                for i in range(min(k, n - j - k)):
                    if (i + j) // (2 * p) == (i + j + k) // (2 * p):
                        pairs.append((i + j, i + j + k))
            k //= 2
        p *= 2
    return pairs


_SORT16 = _sort16_pairs()
_BITONIC16 = [(i, i + dd) for dd in (8, 4, 2, 1) for i in range(PEER_TOPK) if not i & dd]
_CELLS = [(i, j) for i in range(PEER_TOPK) for j in range(PEER_TOPK) if (i + 1) * (j + 1) <= PEER_TOPK]


def _exchange(vals, pairs):
    for a, b in pairs:
        hi, lo = jnp.maximum(vals[a], vals[b]), jnp.minimum(vals[a], vals[b])
        vals[a], vals[b] = hi, lo
    return vals


def _top16_sorted(slabs):
    groups = []
    for s in range(0, len(slabs), PEER_TOPK):
        groups.append(_exchange(list(slabs[s:s + PEER_TOPK]), _SORT16))
    while len(groups) > 1:
        merged = []
        for a, b in zip(groups[::2], groups[1::2]):
            c = [jnp.maximum(a[i], b[PEER_TOPK - 1 - i]) for i in range(PEER_TOPK)]
            merged.append(_exchange(c, _BITONIC16))
        groups = merged
    return groups[0]


def _peer_route_kernel(q_ref, ska0_ref, ska1_ref, skb1_ref, n_out, e1_out, r2_out, e2_out, s1_ref, s2_ref, v2_ref):
    qh = q_ref[...].astype(BF16)
    half = qh.shape[1] // 2
    s1_ref[...] = _dot_nt(ska0_ref[...], qh[:, :half])
    s2_ref[...] = _dot_nt(ska1_ref[...], qh[:, half:])
    nh = PEER_HEADS
    slab = lambda ref, n: ref[n * nh:(n + 1) * nh, :]

    v1 = _top16_sorted([slab(s1_ref, n) for n in range(N_KEYS)])
    v2 = _top16_sorted([slab(s2_ref, n) for n in range(N_KEYS)])

    cand = {c: v1[c[0]] + v2[c[1]] for c in _CELLS}
    cnt = {c: jnp.full_like(v1[0], float((c[0] + 1) * (c[1] + 1) - 1)) for c in _CELLS}
    for c1 in _CELLS:
        for c2 in _CELLS:
            if c1[0] < c2[0] and c1[1] > c2[1]:
                first = cand[c1] >= cand[c2]
                cnt[c2] = cnt[c2] + jnp.where(first, 1.0, 0.0)
                cnt[c1] = cnt[c1] + jnp.where(first, 0.0, 1.0)
    top = cand[(0, 0)]
    z = jnp.zeros_like(top)
    taken = [jnp.zeros_like(top) for _ in range(PEER_TOPK)]
    for c in _CELLS:
        sel = cnt[c] < float(PEER_TOPK)
        z = z + jnp.where(sel, jnp.exp(cand[c] - top), 0.0)
        taken[c[0]] = taken[c[0]] + jnp.where(sel, 1.0, 0.0)
    inv_z = 1.0 / z

    for n in range(N_KEYS):
        s = slab(s1_ref, n)
        cnt_a = jnp.zeros_like(s)
        for i in reversed(range(PEER_TOPK)):
            cnt_a = jnp.where(s == v1[i], taken[i], cnt_a)
        n_out[n * nh:(n + 1) * nh, :] = cnt_a
        e1_out[n * nh:(n + 1) * nh, :] = jnp.exp(s - v1[0]) * inv_z

    for j in range(PEER_TOPK):
        v2_ref[j] = v2[j]
    s2b = _dot_nt(skb1_ref[...], qh[:, half:])
    for h in range(nh):
        s = s2b[h * N_KEYS:(h + 1) * N_KEYS, :]
        rank = jnp.zeros_like(s)
        for j in range(PEER_TOPK):
            rank = jnp.where(v2_ref[j, h:h + 1, :] > s, float(j + 1), rank)
        r2_out[h * N_KEYS:(h + 1) * N_KEYS, :] = rank.astype(BF16)
        e2_out[h * N_KEYS:(h + 1) * N_KEYS, :] = jnp.exp(s - v2_ref[0, h:h + 1, :]).astype(BF16)


def _peer_route(q, ska0, ska1, skb1, tn):
    n, d = q.shape
    rows = PEER_HEADS * N_KEYS
    full = lambda a: pl.BlockSpec(a.shape, lambda i: (0,) * a.ndim)
    col = pl.BlockSpec((rows, tn), lambda i: (0, i))
    return pl.pallas_call(
        _peer_route_kernel,
        grid=(n // tn,),
        in_specs=[pl.BlockSpec((tn, d), lambda i: (i, 0)), full(ska0), full(ska1), full(skb1)],
        out_specs=[col] * 4,
        out_shape=[jax.ShapeDtypeStruct((rows, n), F32), jax.ShapeDtypeStruct((rows, n), F32),
                   jax.ShapeDtypeStruct((rows, n), BF16), jax.ShapeDtypeStruct((rows, n), BF16)],
        scratch_shapes=[pltpu.VMEM((rows, tn), F32), pltpu.VMEM((rows, tn), F32),
                        pltpu.VMEM((PEER_TOPK, PEER_HEADS, tn), F32)],
        compiler_params=_cparams(("arbitrary",)),
        name="peer_route",
    )(q, ska0, ska1, skb1)


def _peer_dense_kernel(n_eb, ct_ref, u_ref, vt_ref, n_ref, e1_ref, r2_ref, e2_ref, out_ref,
                       ht0_ref, ht1_ref, cf_ref, nb_ref, eb_ref):
    s = pl.program_id(0)

    @pl.when(s == 0)
    def _():
        ht0_ref[...] = jnp.zeros_like(ht0_ref)
        ht1_ref[...] = jnp.zeros_like(ht1_ref)

    @pl.when((s <= 1) | ((s - 1) % n_eb == 0))
    def _():
        out_ref[...] = jnp.zeros_like(out_ref)

    nh = PEER_HEADS
    tn = ct_ref.shape[1]
    rg = 16
    a_blk = u_ref.shape[0] // N_KEYS
    chunk = min(tn, 2 * LANES)
    zero = jnp.zeros((rg, chunk), BF16)
    g_half = N_KEYS // rg // 2

    def step(ht_new, ht_old):
        for r in range(a_blk * nh):
            for cols in [slice(lt * LANES, (lt + 1) * LANES) for lt in range(tn // LANES)]:
                for src, dst in ((n_ref, nb_ref), (e1_ref, eb_ref)):
                    dst[r * rg:(r + 1) * rg, cols] = jnp.broadcast_to(src[r:r + 1, cols], (rg, LANES)).astype(BF16)
        for cols in [slice(c * chunk, (c + 1) * chunk) for c in range(tn // chunk)]:
            for al, g0 in [(a, g) for a in range(a_blk) for g in (0, g_half)]:
                if (al, g0) == (a_blk // 2, 0):
                    ht_new[:, cols] = _dot(u_ref[...], ct_ref[:, cols])
                gate = [zero] * g_half
                for h in range(nh):
                    r = al * nh + h
                    n_a, e1_a = nb_ref[r * rg:(r + 1) * rg, cols], eb_ref[r * rg:(r + 1) * rg, cols]
                    for g in range(g_half):
                        rows = slice(h * N_KEYS + (g0 + g) * rg, h * N_KEYS + (g0 + g + 1) * rg)
                        gate[g] = gate[g] + jnp.where(r2_ref[rows, cols] < n_a, e2_ref[rows, cols], zero) * e1_a
                for g in range(g_half):
                    rows = slice(al * N_KEYS + (g0 + g) * rg, al * N_KEYS + (g0 + g + 1) * rg)
                    ht = ht_old[rows, cols].astype(BF16)
                    act = ht * (0.5 * (1.0 + lax.erf(ht * (2.0 ** -0.5))))
                    cf_ref[rows, cols] = act * gate[g]
            out_ref[:, cols] += _dot(vt_ref[...], cf_ref[:, cols])

    @pl.when(s % 2 == 0)
    def _():
        step(ht0_ref, ht1_ref)

    @pl.when(s % 2 == 1)
    def _():
        step(ht1_ref, ht0_ref)


def _peer_dense(ct, u_bf, vt_bf, n_a, e1, r2, e2, tn, a_blk):
    d, n = ct.shape
    e_blk = a_blk * N_KEYS
    n_eb = u_bf.shape[0] // e_blk
    last = (n // tn) * n_eb - 1
    rows = PEER_HEADS * N_KEYS
    pair = lambda s, lag: jnp.clip(s - lag, 0, last)
    tok = lambda s, lag: pair(s, lag) // n_eb
    blk = lambda s, lag: pair(s, lag) % n_eb
    return pl.pallas_call(
        functools.partial(_peer_dense_kernel, n_eb),
        grid=(last + 2,),
        in_specs=[pl.BlockSpec((d, tn), lambda s: (0, tok(s, 0))),
                  pl.BlockSpec((e_blk, d), lambda s: (blk(s, 0), 0)),
                  pl.BlockSpec((d, e_blk), lambda s: (0, blk(s, 1))),
                  pl.BlockSpec((a_blk * PEER_HEADS, tn), lambda s: (blk(s, 1), tok(s, 1))),
                  pl.BlockSpec((a_blk * PEER_HEADS, tn), lambda s: (blk(s, 1), tok(s, 1))),
                  pl.BlockSpec((rows, tn), lambda s: (0, tok(s, 1))),
                  pl.BlockSpec((rows, tn), lambda s: (0, tok(s, 1)))],
        out_specs=pl.BlockSpec((d, tn), lambda s: (0, tok(s, 1))),
        out_shape=jax.ShapeDtypeStruct((d, n), F32),
        scratch_shapes=([pltpu.VMEM((e_blk, tn), F32)] * 2 + [pltpu.VMEM((e_blk, tn), BF16)]
                        + [pltpu.VMEM((a_blk * PEER_HEADS * 16, tn), BF16)] * 2),
        compiler_params=_cparams(("arbitrary",)),
        name="peer_dense",
    )(ct, u_bf, vt_bf, n_a, e1, r2, e2)


def _final_kernel(last_layer, h1_ref, ffnt_ref, p_ref, png_ref, wg_ref, wp_ref, fng_ref, out_ref):
    h2 = h1_ref[...] + ffnt_ref[...].T
    gate = jax.nn.sigmoid(_dot(_rms(h2, png_ref[...]).astype(BF16), wg_ref[...]))
    h3 = h2 + gate * _dot(p_ref[...].astype(BF16), wp_ref[...])
    out_ref[...] = _rms(h3, fng_ref[...]) if last_layer else h3


def _final(h1, ffnt, p, png, wg, wp, fng, tm, last_layer):
    n, d = h1.shape
    full = lambda a: pl.BlockSpec(a.shape, lambda i: (0,) * a.ndim)
    return pl.pallas_call(
        functools.partial(_final_kernel, last_layer),
        grid=(n // tm,),
        in_specs=[pl.BlockSpec((tm, d), lambda i: (i, 0)), pl.BlockSpec((d, tm), lambda i: (0, i)),
                  pl.BlockSpec((tm, p.shape[1]), lambda i: (i, 0)), full(png), full(wg), full(wp), full(fng)],
        out_specs=pl.BlockSpec((tm, d), lambda i: (i, 0)),
        out_shape=jax.ShapeDtypeStruct((n, d), F32),
        compiler_params=_cparams(("arbitrary",)),
        name="final",
    )(h1, ffnt, p, png, wg, wp, fng)


def _tile(n, want):
    t = min(n, want)
    assert n % t == 0, (n, t)
    return t


def kernel(x, p, norm_mix_g, w_in, rwkv_mu, rwkv_w0, rwkv_w_up, rwkv_a0, rwkv_a_up, rwkv_g_up, rwkv_k_k, rwkv_k_a, rwkv_r_k, rwkv_lnx_g, rwkv_lnx_b, rwkv_w_o, conv_dw_w, conv_dw_b, conv_ln_g, conv_ln_b, conv_w_o, w_out, norm_ffn_g, peer_w_q, peer_sub_keys, peer_u, peer_v, ple_w_proj, ple_norm_g, ple_w_gate, final_norm_g):
    bsz, seq, d = x.shape
    n = bsz * seq
    nh, hd = RWKV_HEADS, RWKV_HEAD_DIM
    row = lambda a: a.reshape(1, -1)
    head_of = np.arange(RWKV_WIDTH) // hd
    seg = jnp.asarray(head_of[:, None] == head_of[None, :], BF16)
    h = x.reshape(n, d)

    for l in range(w_in.shape[0]):
        zr, hg, sg = _in_proj(h, row(norm_mix_g[l]), w_in[l].astype(BF16), _tile(seq, 512))
        r, w, k, v, nkk, kka, g, bonus = _rwkv_prep(
            zr, row(rwkv_mu[l]), row(rwkv_w0[l]), rwkv_w_up[l].astype(BF16), row(rwkv_a0[l]),
            rwkv_a_up[l].astype(BF16), rwkv_g_up[l].astype(BF16),
            row(rwkv_k_k[l]), row(rwkv_k_a[l]), row(rwkv_r_k[l]), seg, _tile(seq, 512), seq)

        seqs = lambda a: a.reshape(bsz, seq, RWKV_WIDTH)
        y = _rwkv_scan(seqs(r), seqs(w), seqs(k), seqs(v), seqs(kka), seqs(nkk), _tile(seq, 32))
        y = y.reshape(n, RWKV_WIDTH)

        consts = (row(rwkv_lnx_g[l]), row(rwkv_lnx_b[l]), rwkv_w_o[l].astype(BF16), conv_dw_w[l],
                  row(conv_dw_b[l]), row(conv_ln_g[l]), row(conv_ln_b[l]), conv_w_o[l].astype(BF16),
                  w_out[l].astype(BF16), row(norm_ffn_g[l]),
                  peer_w_q[l].reshape(d, PEER_HEADS, 2, PEER_HALF).transpose(0, 2, 1, 3).reshape(d, -1).astype(BF16))
        h1, ct, q = _mix_out(h, y, g, bonus, hg, sg, seg, consts, _tile(seq, 256), seq)

        sk = peer_sub_keys[l]
        eye = jnp.eye(PEER_HEADS, dtype=sk.dtype)
        blk = lambda half, order: jnp.einsum("hnd,hg->" + order, sk[:, half], eye).reshape(
            PEER_HEADS * N_KEYS, PEER_HEADS * PEER_HALF).astype(BF16)
        n_a, e1, r2, e2 = _peer_route(q, blk(0, "nhgd"), blk(1, "nhgd"), blk(1, "hngd"), _tile(n, 256))
        ffnt = _peer_dense(ct, peer_u[l].astype(BF16), peer_v[l].T.astype(BF16), n_a, e1, r2, e2,
                           _tile(n, 1024), 8)
        h = _final(h1, ffnt, p[l].reshape(n, -1), row(ple_norm_g[l]), ple_w_gate[l].astype(BF16),
                   ple_w_proj[l].astype(BF16), row(final_norm_g), _tile(seq, 256), l == w_in.shape[0] - 1)
    return h.reshape(bsz, seq, d)
```

```python
import functools

import numpy as np
import jax
import jax.numpy as jnp
from jax import lax
from jax.experimental import pallas as pl
from jax.experimental.pallas import tpu as pltpu

F32 = jnp.float32
BF16 = jnp.bfloat16

RWKV_HEADS = 8
RWKV_HEAD_DIM = 64
RWKV_WIDTH = RWKV_HEADS * RWKV_HEAD_DIM
DECAY_LORA = 64
AAA_LORA = 64
GATE_LORA = 128
RWKV_COLS = 3 * RWKV_WIDTH + DECAY_LORA + AAA_LORA + GATE_LORA
GN_EPS = 64e-5
CONV_WIDTH = 512
CONV_K = 31
CONV_HALO = 32
LN_EPS = 1e-5
RMS_EPS = 1e-6
PEER_HEADS = 8
N_KEYS = 128
PEER_HALF = 64
PEER_TOPK = 16

VMEM_LIMIT = 56 * 1024 * 1024
LANES = 128
SUBLANES = 8


def _cparams(sem):
    return pltpu.CompilerParams(dimension_semantics=sem, vmem_limit_bytes=VMEM_LIMIT)


def _dot(a, b):
    return jnp.dot(a, b, preferred_element_type=F32)


def _dot_nt(a, b):
    return lax.dot_general(a, b, (((1,), (1,)), ((), ())), preferred_element_type=F32)


def _dot_split(x, m_bf16):
    hi = x.astype(BF16)
    lo = (x - hi.astype(F32)).astype(BF16)
    return _dot(hi, m_bf16) + _dot(lo, m_bf16)


def _rms(x, g):
    return x * lax.rsqrt(jnp.mean(x * x, axis=-1, keepdims=True) + RMS_EPS) * g


def _in_proj_kernel(x_ref, g_ref, w_ref, zr_ref, hg_ref, sg_ref):
    a = _rms(x_ref[...], g_ref[...]).astype(BF16)
    c0, c1, c2 = RWKV_COLS, RWKV_COLS + CONV_WIDTH, RWKV_COLS + 2 * CONV_WIDTH
    zr_ref[...] = _dot(a, w_ref[:, :c0])
    u = _dot(a, w_ref[:, c0:c1])
    gate = _dot(a, w_ref[:, c1:c2])
    hg_ref[...] = u * jax.nn.sigmoid(gate)
    sg_ref[...] = jax.nn.sigmoid(_dot(a, w_ref[:, c2:])).astype(BF16)


def _in_proj(x, g, w_in, tm):
    n, d = x.shape
    cols = w_in.shape[1]
    gate_cols = cols - RWKV_COLS - 2 * CONV_WIDTH
    row = lambda w: pl.BlockSpec((tm, w), lambda i: (i, 0))
    full = lambda a: pl.BlockSpec(a.shape, lambda i: (0,) * a.ndim)
    return pl.pallas_call(
        _in_proj_kernel,
        grid=(n // tm,),
        in_specs=[row(d), full(g), full(w_in)],
        out_specs=[row(RWKV_COLS), row(CONV_WIDTH), row(gate_cols)],
        out_shape=[jax.ShapeDtypeStruct((n, RWKV_COLS), F32),
                   jax.ShapeDtypeStruct((n, CONV_WIDTH), F32),
                   jax.ShapeDtypeStruct((n, gate_cols), BF16)],
        compiler_params=_cparams(("arbitrary",)),
        name="in_proj",
    )(x, g, w_in)


def _rwkv_prep_kernel(tiles_per_seq, z_ref, mu_ref, w0_ref, wup_ref, a0_ref, aup_ref, gup_ref,
                      kk_ref, ka_ref, rk_ref, seg_ref,
                      r_out, w_out, k_out, v_out, nkk_out, kka_out, g_out, bonus_out, last_ref):
    i = pl.program_id(0)

    @pl.when(i % tiles_per_seq == 0)
    def _():
        last_ref[...] = jnp.zeros_like(last_ref)

    z = z_ref[...]
    tm = z.shape[0]
    rows = lax.broadcasted_iota(jnp.int32, z.shape, 0)
    zs = jnp.where(rows == 0, last_ref[...], pltpu.roll(z, 1, axis=0))
    last_ref[...] = z[tm - 1:tm, :]
    z = z + (zs - z) * mu_ref[...]

    s0 = RWKV_WIDTH
    r = z[:, :s0]
    k = z[:, s0:2 * s0]
    v = z[:, 2 * s0:3 * s0]
    o = 3 * s0
    wl = z[:, o:o + DECAY_LORA]
    al = z[:, o + DECAY_LORA:o + DECAY_LORA + AAA_LORA]
    gl = z[:, o + DECAY_LORA + AAA_LORA:]

    w_log = -jax.nn.softplus(-(w0_ref[...] + _dot_split(jnp.tanh(wl), wup_ref[...]))) - 0.5
    decay = jnp.exp(-jnp.exp(w_log))
    a = jax.nn.sigmoid(a0_ref[...] + _dot_split(al, aup_ref[...]))
    g = _dot_split(jax.nn.sigmoid(gl), gup_ref[...])

    seg = seg_ref[...]
    kk = k * kk_ref[...]
    kk = kk * lax.rsqrt(jnp.maximum(_dot_split(kk * kk, seg), 1e-24))
    k2 = k * (1.0 + (a - 1.0) * ka_ref[...])
    bonus = _dot_split(r * k2 * rk_ref[...], seg) * v

    r_out[...] = r
    w_out[...] = decay
    k_out[...] = k2
    v_out[...] = v
    nkk_out[...] = -kk
    kka_out[...] = kk * a
    g_out[...] = g
    bonus_out[...] = bonus


def _rwkv_prep(zr, mu, w0, w_up, a0, a_up, g_up, k_k, k_a, r_k, seg, tm, seq):
    n = zr.shape[0]
    row = lambda w: pl.BlockSpec((tm, w), lambda i: (i, 0))
    full = lambda a: pl.BlockSpec(a.shape, lambda i: (0,) * a.ndim)
    consts = (mu, w0, w_up, a0, a_up, g_up, k_k, k_a, r_k, seg)
    return pl.pallas_call(
        functools.partial(_rwkv_prep_kernel, seq // tm),
        grid=(n // tm,),
        in_specs=[row(RWKV_COLS)] + [full(c) for c in consts],
        out_specs=[row(RWKV_WIDTH)] * 8,
        out_shape=[jax.ShapeDtypeStruct((n, RWKV_WIDTH), F32)] * 8,
        scratch_shapes=[pltpu.VMEM((1, RWKV_COLS), F32)],
        compiler_params=_cparams(("arbitrary",)),
        name="rwkv_prep",
    )(zr, *consts)


def _rwkv_scan_kernel(r_ref, w_ref, k_ref, v_ref, kka_ref, nkk_ref, y_ref, s_ref, sa_ref, yt_ref, rows_ref, next_ref,
                      pairs_ref):
    @pl.when(pl.program_id(0) == 0)
    def _():
        s_ref[...] = jnp.zeros_like(s_ref)
        yt_ref[...] = jnp.zeros_like(yt_ref)

    bsz, steps, width = r_ref.shape
    nh, hd = RWKV_HEADS, RWKV_HEAD_DIM
    chains = bsz * nh

    def by_pairs(ref, t):
        x = ref[:, t, :]
        return jnp.concatenate([x[:, j * LANES:(j + 1) * LANES] for j in range(width // LANES)], axis=0)

    def chains_on_lanes(pairs):
        xt = pairs.T
        return jnp.concatenate([xt[:hd, :], xt[hd:, :]], axis=1)

    to_lanes = lambda ref, t: chains_on_lanes(by_pairs(ref, t))

    w_i, k_i, kka_i, r_i, nkk_i, v_i = range(6)

    def gather(t):
        t = jnp.minimum(t, steps - 1)
        for i, ref in ((w_i, w_ref), (k_i, k_ref), (kka_i, kka_ref), (r_i, r_ref), (v_i, v_ref)):
            pairs_ref[i] = by_pairs(ref, t)
        pairs_ref[nkk_i] = by_pairs(nkk_ref, jnp.minimum(t + 1, steps - 1))

    def flip():
        for i in range(6):
            next_ref[i] = chains_on_lanes(pairs_ref[i])

    def emit(t):
        y = yt_ref[...]
        yt = jnp.concatenate([y[:, :chains // 2], y[:, chains // 2:]], axis=0).T
        y_ref[:, t, :] = jnp.concatenate([yt[j * bsz:(j + 1) * bsz, :] for j in range(width // LANES)], axis=1)

    rows_ref[nkk_i] = to_lanes(nkk_ref, 0)
    part = [jnp.zeros((hd, chains), F32) for _ in range(4)]
    for kx in range(hd):
        part[kx % 4] = part[kx % 4] + s_ref[kx] * rows_ref[nkk_i, pl.ds(kx, 1), :]
    sa_ref[...] = (part[0] + part[1]) + (part[2] + part[3])
    gather(0)
    flip()
    rows_ref[...] = next_ref[...]
    gather(1)

    def step(t, carry):
        flip()
        row = lambda i, kx: rows_ref[i, pl.ds(kx, 1), :]
        v_t, sa_old = rows_ref[v_i], sa_ref[...]
        y = [jnp.zeros_like(v_t), jnp.zeros_like(v_t)]
        sa = [jnp.zeros_like(v_t), jnp.zeros_like(v_t)]
        for kx in range(hd):
            s_k = s_ref[kx] * row(w_i, kx) + sa_old * row(kka_i, kx) + v_t * row(k_i, kx)
            s_ref[kx] = s_k
            y[kx % 2] = y[kx % 2] + s_k * row(r_i, kx)
            sa[kx % 2] = sa[kx % 2] + s_k * row(nkk_i, kx)
        sa_ref[...] = sa[0] + sa[1]
        emit(jnp.maximum(t - 1, 0))
        yt_ref[...] = y[0] + y[1]
        gather(t + 2)
        rows_ref[...] = next_ref[...]
        return carry

    lax.fori_loop(0, steps, step, 0)
    emit(steps - 1)


def _rwkv_scan(r, w, k, v, kka, nkk, steps):
    bsz, t, width = r.shape
    nh, hd = RWKV_HEADS, RWKV_HEAD_DIM
    blk = pl.BlockSpec((bsz, steps, width), lambda i: (0, i, 0))
    return pl.pallas_call(
        _rwkv_scan_kernel,
        grid=(t // steps,),
        in_specs=[blk] * 6,
        out_specs=blk,
        out_shape=jax.ShapeDtypeStruct((bsz, t, width), F32),
        scratch_shapes=[pltpu.VMEM((hd, hd, bsz * nh), F32), pltpu.VMEM((hd, bsz * nh), F32),
                        pltpu.VMEM((hd, bsz * nh), F32), pltpu.VMEM((6, hd, bsz * nh), F32),
                        pltpu.VMEM((6, hd, bsz * nh), F32), pltpu.VMEM((6, bsz * width // LANES, LANES), F32)],
        compiler_params=_cparams(("arbitrary",)),
        name="rwkv_scan",
    )(r, w, k, v, kka, nkk)


def _mix_out_kernel(tiles_per_seq, x_ref, y_ref, g_ref, bonus_ref, hg_ref, sg_ref, seg_ref,
                    lnxg_ref, lnxb_ref, wor_ref, dww_ref, dwb_ref, clng_ref, clnb_ref, woc_ref,
                    wout_ref, nfg_ref, wq_ref, h1_out, ct_out, q_out, buf_ref):
    i = pl.program_id(0)
    tm = x_ref.shape[0]

    @pl.when(i % tiles_per_seq == 0)
    def _():
        buf_ref[:CONV_HALO, :] = jnp.zeros((CONV_HALO, CONV_WIDTH), F32)
        buf_ref[CONV_HALO + tm:, :] = jnp.zeros((SUBLANES, CONV_WIDTH), F32)

    seg = seg_ref[...]
    y = y_ref[...]
    inv_n = 1.0 / RWKV_HEAD_DIM
    mean = _dot_split(y, seg) * inv_n
    yc = y - mean
    var = _dot_split(yc * yc, seg) * inv_n
    yn = yc * lax.rsqrt(var + GN_EPS) * lnxg_ref[...] + lnxb_ref[...]
    o_r = _dot(((yn + bonus_ref[...]) * g_ref[...]).astype(BF16), wor_ref[...])

    buf_ref[CONV_HALO:CONV_HALO + tm, :] = hg_ref[...]
    acc = jnp.zeros((tm, CONV_WIDTH), F32) + dwb_ref[...]
    for phase in range(SUBLANES):
        part = None
        for j in range(CONV_K):
            off = CONV_HALO - (CONV_K - 1) + j
            if off % SUBLANES == phase:
                base = off - phase
                term = buf_ref[base:base + tm + SUBLANES, :] * dww_ref[j:j + 1, :]
                part = term if part is None else part + term
        acc = acc + part[phase:phase + tm, :]
    buf_ref[:CONV_HALO, :] = buf_ref[tm:tm + CONV_HALO, :]
    mu = jnp.mean(acc, axis=-1, keepdims=True)
    ac = acc - mu
    cv = jnp.mean(ac * ac, axis=-1, keepdims=True)
    hf = ac * lax.rsqrt(cv + LN_EPS) * clng_ref[...] + clnb_ref[...]
    o_c = _dot((hf * jax.nn.sigmoid(hf)).astype(BF16), woc_ref[...])

    d = x_ref.shape[1]
    sg = sg_ref[...].astype(F32)
    mix = sg[:, :d] * o_r + sg[:, d:] * o_c
    h1 = x_ref[...] + _dot(mix.astype(BF16), wout_ref[...])
    h1_out[...] = h1
    c = _rms(h1, nfg_ref[...])
    ct_out[...] = c.T.astype(BF16)
    q_out[...] = _dot(c.astype(BF16), wq_ref[...])


def _mix_out(x, y, g, bonus, hg, sg, seg, consts, tm, seq):
    n, d = x.shape
    row = lambda w: pl.BlockSpec((tm, w), lambda i: (i, 0))
    full = lambda a: pl.BlockSpec(a.shape, lambda i: (0,) * a.ndim)
    return pl.pallas_call(
        functools.partial(_mix_out_kernel, seq // tm),
        grid=(n // tm,),
        in_specs=[row(d), row(RWKV_WIDTH), row(RWKV_WIDTH), row(RWKV_WIDTH), row(CONV_WIDTH), row(2 * d),
                  full(seg)] + [full(c) for c in consts],
        out_specs=[row(d), pl.BlockSpec((d, tm), lambda i: (0, i)), row(d)],
        out_shape=[jax.ShapeDtypeStruct((n, d), F32), jax.ShapeDtypeStruct((d, n), BF16),
                   jax.ShapeDtypeStruct((n, d), F32)],
        scratch_shapes=[pltpu.VMEM((tm + CONV_HALO + SUBLANES, CONV_WIDTH), F32)],
        compiler_params=_cparams(("arbitrary",)),
        name="mix_out",
    )(x, y, g, bonus, hg, sg, seg, *consts)


def _sort16_pairs():
    n, pairs, p = PEER_TOPK, [], 1
    while p < n:
        k = p
        while k >= 1:
            for j in range(k % p, n - k, 2 * k):
                for i in range(min(k, n - j - k)):
                    if (i + j) // (2 * p) == (i + j + k) // (2 * p):
                        pairs.append((i + j, i + j + k))
            k //= 2
        p *= 2
    return pairs


_SORT16 = _sort16_pairs()
_BITONIC16 = [(i, i + dd) for dd in (8, 4, 2, 1) for i in range(PEER_TOPK) if not i & dd]
_CELLS = [(i, j) for i in range(PEER_TOPK) for j in range(PEER_TOPK) if (i + 1) * (j + 1) <= PEER_TOPK]


def _exchange(vals, pairs):
    for a, b in pairs:
        hi, lo = jnp.maximum(vals[a], vals[b]), jnp.minimum(vals[a], vals[b])
        vals[a], vals[b] = hi, lo
    return vals


def _top16_sorted(slabs):
    groups = []
    for s in range(0, len(slabs), PEER_TOPK):
        groups.append(_exchange(list(slabs[s:s + PEER_TOPK]), _SORT16))
    while len(groups) > 1:
        merged = []
        for a, b in zip(groups[::2], groups[1::2]):
            c = [jnp.maximum(a[i], b[PEER_TOPK - 1 - i]) for i in range(PEER_TOPK)]
            merged.append(_exchange(c, _BITONIC16))
        groups = merged
    return groups[0]


def _peer_route_kernel(q_ref, ska0_ref, ska1_ref, skb1_ref, n_out, e1_out, r2_out, e2_out, s1_ref, s2_ref, v2_ref):
    qh = q_ref[...].astype(BF16)
    half = qh.shape[1] // 2
    s1_ref[...] = _dot_nt(ska0_ref[...], qh[:, :half])
    s2_ref[...] = _dot_nt(ska1_ref[...], qh[:, half:])
    nh = PEER_HEADS
    slab = lambda ref, n: ref[n * nh:(n + 1) * nh, :]

    v1 = _top16_sorted([slab(s1_ref, n) for n in range(N_KEYS)])
    v2 = _top16_sorted([slab(s2_ref, n) for n in range(N_KEYS)])

    cand = {c: v1[c[0]] + v2[c[1]] for c in _CELLS}
    cnt = {c: jnp.full_like(v1[0], float((c[0] + 1) * (c[1] + 1) - 1)) for c in _CELLS}
    for c1 in _CELLS:
        for c2 in _CELLS:
            if c1[0] < c2[0] and c1[1] > c2[1]:
                first = cand[c1] >= cand[c2]
                cnt[c2] = cnt[c2] + jnp.where(first, 1.0, 0.0)
                cnt[c1] = cnt[c1] + jnp.where(first, 0.0, 1.0)
    top = cand[(0, 0)]
    z = jnp.zeros_like(top)
    taken = [jnp.zeros_like(top) for _ in range(PEER_TOPK)]
    for c in _CELLS:
        sel = cnt[c] < float(PEER_TOPK)
        z = z + jnp.where(sel, jnp.exp(cand[c] - top), 0.0)
        taken[c[0]] = taken[c[0]] + jnp.where(sel, 1.0, 0.0)
    inv_z = 1.0 / z

    for n in range(N_KEYS):
        s = slab(s1_ref, n)
        cnt_a = jnp.zeros_like(s)
        for i in reversed(range(PEER_TOPK)):
            cnt_a = jnp.where(s == v1[i], taken[i], cnt_a)
        n_out[n * nh:(n + 1) * nh, :] = cnt_a
        e1_out[n * nh:(n + 1) * nh, :] = jnp.exp(s - v1[0]) * inv_z

    for j in range(PEER_TOPK):
        v2_ref[j] = v2[j]
    s2b = _dot_nt(skb1_ref[...], qh[:, half:])
    for h in range(nh):
        s = s2b[h * N_KEYS:(h + 1) * N_KEYS, :]
        rank = jnp.zeros_like(s)
        for j in range(PEER_TOPK):
            rank = jnp.where(v2_ref[j, h:h + 1, :] > s, float(j + 1), rank)
        r2_out[h * N_KEYS:(h + 1) * N_KEYS, :] = rank.astype(BF16)
        e2_out[h * N_KEYS:(h + 1) * N_KEYS, :] = jnp.exp(s - v2_ref[0, h:h + 1, :]).astype(BF16)


def _peer_route(q, ska0, ska1, skb1, tn):
    n, d = q.shape
    rows = PEER_HEADS * N_KEYS
    full = lambda a: pl.BlockSpec(a.shape, lambda i: (0,) * a.ndim)
    col = pl.BlockSpec((rows, tn), lambda i: (0, i))
    return pl.pallas_call(
        _peer_route_kernel,
        grid=(n // tn,),
        in_specs=[pl.BlockSpec((tn, d), lambda i: (i, 0)), full(ska0), full(ska1), full(skb1)],
        out_specs=[col] * 4,
        out_shape=[jax.ShapeDtypeStruct((rows, n), F32), jax.ShapeDtypeStruct((rows, n), F32),
                   jax.ShapeDtypeStruct((rows, n), BF16), jax.ShapeDtypeStruct((rows, n), BF16)],
        scratch_shapes=[pltpu.VMEM((rows, tn), F32), pltpu.VMEM((rows, tn), F32),
                        pltpu.VMEM((PEER_TOPK, PEER_HEADS, tn), F32)],
        compiler_params=_cparams(("arbitrary",)),
        name="peer_route",
    )(q, ska0, ska1, skb1)


def _peer_dense_kernel(n_eb, ct_ref, u_ref, vt_ref, n_ref, e1_ref, r2_ref, e2_ref, out_ref,
                       ht0_ref, ht1_ref, cf_ref, nb_ref, eb_ref):
    s = pl.program_id(0)

    @pl.when(s == 0)
    def _():
        ht0_ref[...] = jnp.zeros_like(ht0_ref)
        ht1_ref[...] = jnp.zeros_like(ht1_ref)

    @pl.when((s <= 1) | ((s - 1) % n_eb == 0))
    def _():
        out_ref[...] = jnp.zeros_like(out_ref)

    nh = PEER_HEADS
    tn = ct_ref.shape[1]
    rg = 16
    a_blk = u_ref.shape[0] // N_KEYS
    chunk = min(tn, 2 * LANES)
    zero = jnp.zeros((rg, chunk), BF16)
    g_half = N_KEYS // rg // 2

    def step(ht_new, ht_old):
        ht_new[...] = _dot(u_ref[...], ct_ref[...])
        for r in range(a_blk * nh):
            for cols in [slice(lt * LANES, (lt + 1) * LANES) for lt in range(tn // LANES)]:
                for src, dst in ((n_ref, nb_ref), (e1_ref, eb_ref)):
                    dst[r * rg:(r + 1) * rg, cols] = jnp.broadcast_to(src[r:r + 1, cols], (rg, LANES)).astype(BF16)
        for cols in [slice(c * chunk, (c + 1) * chunk) for c in range(tn // chunk)]:
            for al, g0 in [(a, g) for a in range(a_blk) for g in (0, g_half)]:
                gate = [zero] * g_half
                for h in range(nh):
                    r = al * nh + h
                    n_a, e1_a = nb_ref[r * rg:(r + 1) * rg, cols], eb_ref[r * rg:(r + 1) * rg, cols]
                    for g in range(g_half):
                        rows = slice(h * N_KEYS + (g0 + g) * rg, h * N_KEYS + (g0 + g + 1) * rg)
                        gate[g] = gate[g] + jnp.where(r2_ref[rows, cols] < n_a, e2_ref[rows, cols], zero) * e1_a
                for g in range(g_half):
                    rows = slice(al * N_KEYS + (g0 + g) * rg, al * N_KEYS + (g0 + g + 1) * rg)
                    ht = ht_old[rows, cols].astype(BF16)
                    act = ht * (0.5 * (1.0 + lax.erf(ht * (2.0 ** -0.5))))
                    cf_ref[rows, cols] = act * gate[g]
            out_ref[:, cols] += _dot(vt_ref[...], cf_ref[:, cols])

    @pl.when(s % 2 == 0)
    def _():
        step(ht0_ref, ht1_ref)

    @pl.when(s % 2 == 1)
    def _():
        step(ht1_ref, ht0_ref)


def _peer_dense(ct, u_bf, vt_bf, n_a, e1, r2, e2, tn, a_blk):
    d, n = ct.shape
    e_blk = a_blk * N_KEYS
    n_eb = u_bf.shape[0] // e_blk
    last = (n // tn) * n_eb - 1
    rows = PEER_HEADS * N_KEYS
    pair = lambda s, lag: jnp.clip(s - lag, 0, last)
    tok = lambda s, lag: pair(s, lag) // n_eb
    blk = lambda s, lag: pair(s, lag) % n_eb
    return pl.pallas_call(
        functools.partial(_peer_dense_kernel, n_eb),
        grid=(last + 2,),
        in_specs=[pl.BlockSpec((d, tn), lambda s: (0, tok(s, 0))),
                  pl.BlockSpec((e_blk, d), lambda s: (blk(s, 0), 0)),
                  pl.BlockSpec((d, e_blk), lambda s: (0, blk(s, 1))),
                  pl.BlockSpec((a_blk * PEER_HEADS, tn), lambda s: (blk(s, 1), tok(s, 1))),
                  pl.BlockSpec((a_blk * PEER_HEADS, tn), lambda s: (blk(s, 1), tok(s, 1))),
                  pl.BlockSpec((rows, tn), lambda s: (0, tok(s, 1))),
                  pl.BlockSpec((rows, tn), lambda s: (0, tok(s, 1)))],
        out_specs=pl.BlockSpec((d, tn), lambda s: (0, tok(s, 1))),
        out_shape=jax.ShapeDtypeStruct((d, n), F32),
        scratch_shapes=([pltpu.VMEM((e_blk, tn), F32)] * 2 + [pltpu.VMEM((e_blk, tn), BF16)]
                        + [pltpu.VMEM((a_blk * PEER_HEADS * 16, tn), BF16)] * 2),
        compiler_params=_cparams(("arbitrary",)),
        name="peer_dense",
    )(ct, u_bf, vt_bf, n_a, e1, r2, e2)


def _final_kernel(last_layer, h1_ref, ffnt_ref, p_ref, png_ref, wg_ref, wp_ref, fng_ref, out_ref):
    h2 = h1_ref[...] + ffnt_ref[...].T
    gate = jax.nn.sigmoid(_dot(_rms(h2, png_ref[...]).astype(BF16), wg_ref[...]))
    h3 = h2 + gate * _dot(p_ref[...].astype(BF16), wp_ref[...])
    out_ref[...] = _rms(h3, fng_ref[...]) if last_layer else h3


def _final(h1, ffnt, p, png, wg, wp, fng, tm, last_layer):
    n, d = h1.shape
    full = lambda a: pl.BlockSpec(a.shape, lambda i: (0,) * a.ndim)
    return pl.pallas_call(
        functools.partial(_final_kernel, last_layer),
        grid=(n // tm,),
        in_specs=[pl.BlockSpec((tm, d), lambda i: (i, 0)), pl.BlockSpec((d, tm), lambda i: (0, i)),
                  pl.BlockSpec((tm, p.shape[1]), lambda i: (i, 0)), full(png), full(wg), full(wp), full(fng)],
        out_specs=pl.BlockSpec((tm, d), lambda i: (i, 0)),
        out_shape=jax.ShapeDtypeStruct((n, d), F32),
        compiler_params=_cparams(("arbitrary",)),
        name="final",
    )(h1, ffnt, p, png, wg, wp, fng)


def _tile(n, want):
    t = min(n, want)
    assert n % t == 0, (n, t)
    return t


def kernel(x, p, norm_mix_g, w_in, rwkv_mu, rwkv_w0, rwkv_w_up, rwkv_a0, rwkv_a_up, rwkv_g_up, rwkv_k_k, rwkv_k_a, rwkv_r_k, rwkv_lnx_g, rwkv_lnx_b, rwkv_w_o, conv_dw_w, conv_dw_b, conv_ln_g, conv_ln_b, conv_w_o, w_out, norm_ffn_g, peer_w_q, peer_sub_keys, peer_u, peer_v, ple_w_proj, ple_norm_g, ple_w_gate, final_norm_g):
    bsz, seq, d = x.shape
    n = bsz * seq
    nh, hd = RWKV_HEADS, RWKV_HEAD_DIM
    row = lambda a: a.reshape(1, -1)
    head_of = np.arange(RWKV_WIDTH) // hd
    seg = jnp.asarray(head_of[:, None] == head_of[None, :], BF16)
    h = x.reshape(n, d)

    for l in range(w_in.shape[0]):
        zr, hg, sg = _in_proj(h, row(norm_mix_g[l]), w_in[l].astype(BF16), _tile(seq, 512))
        r, w, k, v, nkk, kka, g, bonus = _rwkv_prep(
            zr, row(rwkv_mu[l]), row(rwkv_w0[l]), rwkv_w_up[l].astype(BF16), row(rwkv_a0[l]),
            rwkv_a_up[l].astype(BF16), rwkv_g_up[l].astype(BF16),
            row(rwkv_k_k[l]), row(rwkv_k_a[l]), row(rwkv_r_k[l]), seg, _tile(seq, 512), seq)

        seqs = lambda a: a.reshape(bsz, seq, RWKV_WIDTH)
        y = _rwkv_scan(seqs(r), seqs(w), seqs(k), seqs(v), seqs(kka), seqs(nkk), _tile(seq, 32))
        y = y.reshape(n, RWKV_WIDTH)

        consts = (row(rwkv_lnx_g[l]), row(rwkv_lnx_b[l]), rwkv_w_o[l].astype(BF16), conv_dw_w[l],
                  row(conv_dw_b[l]), row(conv_ln_g[l]), row(conv_ln_b[l]), conv_w_o[l].astype(BF16),
                  w_out[l].astype(BF16), row(norm_ffn_g[l]),
                  peer_w_q[l].reshape(d, PEER_HEADS, 2, PEER_HALF).transpose(0, 2, 1, 3).reshape(d, -1).astype(BF16))
        h1, ct, q = _mix_out(h, y, g, bonus, hg, sg, seg, consts, _tile(seq, 256), seq)

        sk = peer_sub_keys[l]
        eye = jnp.eye(PEER_HEADS, dtype=sk.dtype)
        blk = lambda half, order: jnp.einsum("hnd,hg->" + order, sk[:, half], eye).reshape(
            PEER_HEADS * N_KEYS, PEER_HEADS * PEER_HALF).astype(BF16)
        n_a, e1, r2, e2 = _peer_route(q, blk(0, "nhgd"), blk(1, "nhgd"), blk(1, "hngd"), _tile(n, 256))
        ffnt = _peer_dense(ct, peer_u[l].astype(BF16), peer_v[l].T.astype(BF16), n_a, e1, r2, e2,
                           _tile(n, 1024), 8)
        h = _final(h1, ffnt, p[l].reshape(n, -1), row(ple_norm_g[l]), ple_w_gate[l].astype(BF16),
                   ple_w_proj[l].astype(BF16), row(final_norm_g), _tile(seq, 256), l == w_in.shape[0] - 1)
    return h.reshape(bsz, seq, d)
```

```python
import functools

import numpy as np
import jax
import jax.numpy as jnp
from jax import lax
from jax.experimental import pallas as pl
from jax.experimental.pallas import tpu as pltpu

F32 = jnp.float32
BF16 = jnp.bfloat16

RWKV_HEADS = 8
RWKV_HEAD_DIM = 64
RWKV_WIDTH = RWKV_HEADS * RWKV_HEAD_DIM
DECAY_LORA = 64
AAA_LORA = 64
GATE_LORA = 128
RWKV_COLS = 3 * RWKV_WIDTH + DECAY_LORA + AAA_LORA + GATE_LORA
GN_EPS = 64e-5
CONV_WIDTH = 512
CONV_K = 31
CONV_HALO = 32
LN_EPS = 1e-5
RMS_EPS = 1e-6
PEER_HEADS = 8
N_KEYS = 128
PEER_HALF = 64
PEER_TOPK = 16

VMEM_LIMIT = 56 * 1024 * 1024
LANES = 128
SUBLANES = 8
BF16_ROWS = 16


def _cparams(sem):
    return pltpu.CompilerParams(dimension_semantics=sem, vmem_limit_bytes=VMEM_LIMIT)


def _dot(a, b):
    return jnp.dot(a, b, preferred_element_type=F32)


def _dot_nt(a, b):
    return lax.dot_general(a, b, (((1,), (1,)), ((), ())), preferred_element_type=F32)


def _dot_split(x, m_bf16):
    hi = x.astype(BF16)
    lo = (x - hi.astype(F32)).astype(BF16)
    return _dot(hi, m_bf16) + _dot(lo, m_bf16)


def _rms(x, g):
    return x * lax.rsqrt(jnp.mean(x * x, axis=-1, keepdims=True) + RMS_EPS) * g


def _in_proj_kernel(x_ref, g_ref, w_ref, zr_ref, hg_ref, sg_ref):
    a = _rms(x_ref[...], g_ref[...]).astype(BF16)
    c0, c1, c2 = RWKV_COLS, RWKV_COLS + CONV_WIDTH, RWKV_COLS + 2 * CONV_WIDTH
    zr_ref[...] = _dot(a, w_ref[:, :c0])
    u = _dot(a, w_ref[:, c0:c1])
    gate = _dot(a, w_ref[:, c1:c2])
    hg_ref[...] = u * jax.nn.sigmoid(gate)
    sg_ref[...] = jax.nn.sigmoid(_dot(a, w_ref[:, c2:])).astype(BF16)


def _in_proj(x, g, w_in, tm):
    n, d = x.shape
    cols = w_in.shape[1]
    gate_cols = cols - RWKV_COLS - 2 * CONV_WIDTH
    row = lambda w: pl.BlockSpec((tm, w), lambda i: (i, 0))
    full = lambda a: pl.BlockSpec(a.shape, lambda i: (0,) * a.ndim)
    return pl.pallas_call(
        _in_proj_kernel,
        grid=(n // tm,),
        in_specs=[row(d), full(g), full(w_in)],
        out_specs=[row(RWKV_COLS), row(CONV_WIDTH), row(gate_cols)],
        out_shape=[jax.ShapeDtypeStruct((n, RWKV_COLS), F32),
                   jax.ShapeDtypeStruct((n, CONV_WIDTH), F32),
                   jax.ShapeDtypeStruct((n, gate_cols), BF16)],
        compiler_params=_cparams(("arbitrary",)),
        name="in_proj",
    )(x, g, w_in)


def _rwkv_prep_kernel(tiles_per_seq, z_ref, mu_ref, w0_ref, wup_ref, a0_ref, aup_ref, gup_ref,
                      kk_ref, ka_ref, rk_ref, seg_ref,
                      r_out, w_out, k_out, v_out, nkk_out, kka_out, g_out, bonus_out, last_ref):
    i = pl.program_id(0)

    @pl.when(i % tiles_per_seq == 0)
    def _():
        last_ref[...] = jnp.zeros_like(last_ref)

    z = z_ref[...]
    tm = z.shape[0]
    rows = lax.broadcasted_iota(jnp.int32, z.shape, 0)
    zs = jnp.where(rows == 0, last_ref[...], pltpu.roll(z, 1, axis=0))
    last_ref[...] = z[tm - 1:tm, :]
    z = z + (zs - z) * mu_ref[...]

    s0 = RWKV_WIDTH
    r = z[:, :s0]
    k = z[:, s0:2 * s0]
    v = z[:, 2 * s0:3 * s0]
    o = 3 * s0
    wl = z[:, o:o + DECAY_LORA]
    al = z[:, o + DECAY_LORA:o + DECAY_LORA + AAA_LORA]
    gl = z[:, o + DECAY_LORA + AAA_LORA:]

    w_log = -jax.nn.softplus(-(w0_ref[...] + _dot_split(jnp.tanh(wl), wup_ref[...]))) - 0.5
    decay = jnp.exp(-jnp.exp(w_log))
    a = jax.nn.sigmoid(a0_ref[...] + _dot_split(al, aup_ref[...]))
    g = _dot_split(jax.nn.sigmoid(gl), gup_ref[...])

    seg = seg_ref[...]
    kk = k * kk_ref[...]
    kk = kk * lax.rsqrt(jnp.maximum(_dot_split(kk * kk, seg), 1e-24))
    k2 = k * (1.0 + (a - 1.0) * ka_ref[...])
    bonus = _dot_split(r * k2 * rk_ref[...], seg) * v

    r_out[...] = r
    w_out[...] = decay
    k_out[...] = k2
    v_out[...] = v
    nkk_out[...] = -kk
    kka_out[...] = kk * a
    g_out[...] = g
    bonus_out[...] = bonus


def _rwkv_prep(zr, mu, w0, w_up, a0, a_up, g_up, k_k, k_a, r_k, seg, tm, seq):
    n = zr.shape[0]
    row = lambda w: pl.BlockSpec((tm, w), lambda i: (i, 0))
    full = lambda a: pl.BlockSpec(a.shape, lambda i: (0,) * a.ndim)
    consts = (mu, w0, w_up, a0, a_up, g_up, k_k, k_a, r_k, seg)
    return pl.pallas_call(
        functools.partial(_rwkv_prep_kernel, seq // tm),
        grid=(n // tm,),
        in_specs=[row(RWKV_COLS)] + [full(c) for c in consts],
        out_specs=[row(RWKV_WIDTH)] * 8,
        out_shape=[jax.ShapeDtypeStruct((n, RWKV_WIDTH), F32)] * 8,
        scratch_shapes=[pltpu.VMEM((1, RWKV_COLS), F32)],
        compiler_params=_cparams(("arbitrary",)),
        name="rwkv_prep",
    )(zr, *consts)


def _rwkv_scan_kernel(r_ref, w_ref, k_ref, v_ref, kka_ref, nkk_ref, y_ref, s_ref, sa_ref, yt_ref, rows_ref, next_ref,
                      pairs_ref):
    @pl.when(pl.program_id(0) == 0)
    def _():
        s_ref[...] = jnp.zeros_like(s_ref)
        yt_ref[...] = jnp.zeros_like(yt_ref)

    bsz, steps, width = r_ref.shape
    nh, hd = RWKV_HEADS, RWKV_HEAD_DIM
    chains = bsz * nh

    def by_pairs(ref, t):
        x = ref[:, t, :]
        return jnp.concatenate([x[:, j * LANES:(j + 1) * LANES] for j in range(width // LANES)], axis=0)

    def chains_on_lanes(pairs):
        xt = pairs.T
        return jnp.concatenate([xt[:hd, :], xt[hd:, :]], axis=1)

    to_lanes = lambda ref, t: chains_on_lanes(by_pairs(ref, t))

    w_i, k_i, kka_i, r_i, nkk_i, v_i = range(6)

    def gather(t):
        t = jnp.minimum(t, steps - 1)
        for i, ref in ((w_i, w_ref), (k_i, k_ref), (kka_i, kka_ref), (r_i, r_ref), (v_i, v_ref)):
            pairs_ref[i] = by_pairs(ref, t)
        pairs_ref[nkk_i] = by_pairs(nkk_ref, jnp.minimum(t + 1, steps - 1))

    def flip():
        for i in range(6):
            next_ref[i] = chains_on_lanes(pairs_ref[i])

    def emit(t):
        y = yt_ref[...]
        yt = jnp.concatenate([y[:, :chains // 2], y[:, chains // 2:]], axis=0).T
        y_ref[:, t, :] = jnp.concatenate([yt[j * bsz:(j + 1) * bsz, :] for j in range(width // LANES)], axis=1)

    rows_ref[nkk_i] = to_lanes(nkk_ref, 0)
    part = [jnp.zeros((hd, chains), F32) for _ in range(4)]
    for kx in range(hd):
        part[kx % 4] = part[kx % 4] + s_ref[kx] * rows_ref[nkk_i, pl.ds(kx, 1), :]
    sa_ref[...] = (part[0] + part[1]) + (part[2] + part[3])
    gather(0)
    flip()
    rows_ref[...] = next_ref[...]
    gather(1)

    def step(t, carry):
        flip()
        row = lambda i, kx: rows_ref[i, pl.ds(kx, 1), :]
        v_t, sa_old = rows_ref[v_i], sa_ref[...]
        y = [jnp.zeros_like(v_t), jnp.zeros_like(v_t)]
        sa = [jnp.zeros_like(v_t), jnp.zeros_like(v_t)]
        for kx in range(hd):
            s_k = s_ref[kx] * row(w_i, kx) + sa_old * row(kka_i, kx) + v_t * row(k_i, kx)
            s_ref[kx] = s_k
            y[kx % 2] = y[kx % 2] + s_k * row(r_i, kx)
            sa[kx % 2] = sa[kx % 2] + s_k * row(nkk_i, kx)
        sa_ref[...] = sa[0] + sa[1]
        emit(jnp.maximum(t - 1, 0))
        yt_ref[...] = y[0] + y[1]
        gather(t + 2)
        rows_ref[...] = next_ref[...]
        return carry

    lax.fori_loop(0, steps, step, 0)
    emit(steps - 1)


def _rwkv_scan(r, w, k, v, kka, nkk, steps):
    bsz, t, width = r.shape
    nh, hd = RWKV_HEADS, RWKV_HEAD_DIM
    blk = pl.BlockSpec((bsz, steps, width), lambda i: (0, i, 0))
    return pl.pallas_call(
        _rwkv_scan_kernel,
        grid=(t // steps,),
        in_specs=[blk] * 6,
        out_specs=blk,
        out_shape=jax.ShapeDtypeStruct((bsz, t, width), F32),
        scratch_shapes=[pltpu.VMEM((hd, hd, bsz * nh), F32), pltpu.VMEM((hd, bsz * nh), F32),
                        pltpu.VMEM((hd, bsz * nh), F32), pltpu.VMEM((6, hd, bsz * nh), F32),
                        pltpu.VMEM((6, hd, bsz * nh), F32), pltpu.VMEM((6, bsz * width // LANES, LANES), F32)],
        compiler_params=_cparams(("arbitrary",)),
        name="rwkv_scan",
    )(r, w, k, v, kka, nkk)


def _mix_out_kernel(tiles_per_seq, x_ref, y_ref, g_ref, bonus_ref, hg_ref, sg_ref, seg_ref,
                    lnxg_ref, lnxb_ref, wor_ref, dww_ref, dwb_ref, clng_ref, clnb_ref, woc_ref,
                    wout_ref, nfg_ref, wq_ref, h1_out, ct_out, q_out, buf_ref):
    i = pl.program_id(0)
    tm = x_ref.shape[0]

    @pl.when(i % tiles_per_seq == 0)
    def _():
        buf_ref[:CONV_HALO, :] = jnp.zeros((CONV_HALO, CONV_WIDTH), F32)
        buf_ref[CONV_HALO + tm:, :] = jnp.zeros((SUBLANES, CONV_WIDTH), F32)

    seg = seg_ref[...]
    y = y_ref[...]
    inv_n = 1.0 / RWKV_HEAD_DIM
    mean = _dot_split(y, seg) * inv_n
    yc = y - mean
    var = _dot_split(yc * yc, seg) * inv_n
    yn = yc * lax.rsqrt(var + GN_EPS) * lnxg_ref[...] + lnxb_ref[...]
    o_r = _dot(((yn + bonus_ref[...]) * g_ref[...]).astype(BF16), wor_ref[...])

    buf_ref[CONV_HALO:CONV_HALO + tm, :] = hg_ref[...]
    acc = jnp.zeros((tm, CONV_WIDTH), F32) + dwb_ref[...]
    for phase in range(SUBLANES):
        part = None
        for j in range(CONV_K):
            off = CONV_HALO - (CONV_K - 1) + j
            if off % SUBLANES == phase:
                base = off - phase
                term = buf_ref[base:base + tm + SUBLANES, :] * dww_ref[j:j + 1, :]
                part = term if part is None else part + term
        acc = acc + part[phase:phase + tm, :]
    buf_ref[:CONV_HALO, :] = buf_ref[tm:tm + CONV_HALO, :]
    mu = jnp.mean(acc, axis=-1, keepdims=True)
    ac = acc - mu
    cv = jnp.mean(ac * ac, axis=-1, keepdims=True)
    hf = ac * lax.rsqrt(cv + LN_EPS) * clng_ref[...] + clnb_ref[...]
    o_c = _dot((hf * jax.nn.sigmoid(hf)).astype(BF16), woc_ref[...])

    d = x_ref.shape[1]
    sg = sg_ref[...].astype(F32)
    mix = sg[:, :d] * o_r + sg[:, d:] * o_c
    h1 = x_ref[...] + _dot(mix.astype(BF16), wout_ref[...])
    h1_out[...] = h1
    c = _rms(h1, nfg_ref[...])
    ct_out[...] = c.T.astype(BF16)
    q_out[...] = _dot(c.astype(BF16), wq_ref[...])


def _mix_out(x, y, g, bonus, hg, sg, seg, consts, tm, seq):
    n, d = x.shape
    row = lambda w: pl.BlockSpec((tm, w), lambda i: (i, 0))
    full = lambda a: pl.BlockSpec(a.shape, lambda i: (0,) * a.ndim)
    return pl.pallas_call(
        functools.partial(_mix_out_kernel, seq // tm),
        grid=(n // tm,),
        in_specs=[row(d), row(RWKV_WIDTH), row(RWKV_WIDTH), row(RWKV_WIDTH), row(CONV_WIDTH), row(2 * d),
                  full(seg)] + [full(c) for c in consts],
        out_specs=[row(d), pl.BlockSpec((d, tm), lambda i: (0, i)), row(d)],
        out_shape=[jax.ShapeDtypeStruct((n, d), F32), jax.ShapeDtypeStruct((d, n), BF16),
                   jax.ShapeDtypeStruct((n, d), F32)],
        scratch_shapes=[pltpu.VMEM((tm + CONV_HALO + SUBLANES, CONV_WIDTH), F32)],
        compiler_params=_cparams(("arbitrary",)),
        name="mix_out",
    )(x, y, g, bonus, hg, sg, seg, *consts)


def _sort16_pairs():
    n, pairs, p = PEER_TOPK, [], 1
    while p < n:
        k = p
        while k >= 1:
            for j in range(k % p, n - k, 2 * k):
                for i in range(min(k, n - j - k)):
                    if (i + j) // (2 * p) == (i + j + k) // (2 * p):
                        pairs.append((i + j, i + j + k))
            k //= 2
        p *= 2
    return pairs


_SORT16 = _sort16_pairs()
_BITONIC16 = [(i, i + dd) for dd in (8, 4, 2, 1) for i in range(PEER_TOPK) if not i & dd]
_CELLS = [(i, j) for i in range(PEER_TOPK) for j in range(PEER_TOPK) if (i + 1) * (j + 1) <= PEER_TOPK]


def _exchange(vals, pairs):
    for a, b in pairs:
        hi, lo = jnp.maximum(vals[a], vals[b]), jnp.minimum(vals[a], vals[b])
        vals[a], vals[b] = hi, lo
    return vals


def _top16_sorted(slabs):
    groups = []
    for s in range(0, len(slabs), PEER_TOPK):
        groups.append(_exchange(list(slabs[s:s + PEER_TOPK]), _SORT16))
    while len(groups) > 1:
        merged = []
        for a, b in zip(groups[::2], groups[1::2]):
            c = [jnp.maximum(a[i], b[PEER_TOPK - 1 - i]) for i in range(PEER_TOPK)]
            merged.append(_exchange(c, _BITONIC16))
        groups = merged
    return groups[0]


def _peer_route_kernel(q_ref, ska0_ref, ska1_ref, skb1_ref, n_out, e1_out, r2_out, e2_out, s1_ref, s2_ref, v2_ref):
    qh = q_ref[...].astype(BF16)
    half = qh.shape[1] // 2
    s1_ref[...] = _dot_nt(ska0_ref[...], qh[:, :half])
    s2_ref[...] = _dot_nt(ska1_ref[...], qh[:, half:])
    nh = PEER_HEADS
    slab = lambda ref, n: ref[n * nh:(n + 1) * nh, :]

    v1 = _top16_sorted([slab(s1_ref, n) for n in range(N_KEYS)])
    v2 = _top16_sorted([slab(s2_ref, n) for n in range(N_KEYS)])

    cand = {c: v1[c[0]] + v2[c[1]] for c in _CELLS}
    cnt = {c: jnp.full_like(v1[0], float((c[0] + 1) * (c[1] + 1) - 1)) for c in _CELLS}
    for c1 in _CELLS:
        for c2 in _CELLS:
            if c1[0] < c2[0] and c1[1] > c2[1]:
                first = cand[c1] >= cand[c2]
                cnt[c2] = cnt[c2] + jnp.where(first, 1.0, 0.0)
                cnt[c1] = cnt[c1] + jnp.where(first, 0.0, 1.0)
    top = cand[(0, 0)]
    z = jnp.zeros_like(top)
    taken = [jnp.zeros_like(top) for _ in range(PEER_TOPK)]
    for c in _CELLS:
        sel = cnt[c] < float(PEER_TOPK)
        z = z + jnp.where(sel, jnp.exp(cand[c] - top), 0.0)
        taken[c[0]] = taken[c[0]] + jnp.where(sel, 1.0, 0.0)
    inv_z = 1.0 / z

    for n in range(N_KEYS):
        s = slab(s1_ref, n)
        cnt_a = jnp.zeros_like(s)
        for i in reversed(range(PEER_TOPK)):
            cnt_a = jnp.where(s == v1[i], taken[i], cnt_a)
        n_out[n * nh:(n + 1) * nh, :] = cnt_a
        e1_out[n * nh:(n + 1) * nh, :] = jnp.exp(s - v1[0]) * inv_z

    for j in range(PEER_TOPK):
        v2_ref[j] = v2[j]
    s2b = _dot_nt(skb1_ref[...], qh[:, half:])
    for h in range(nh):
        s = s2b[h * N_KEYS:(h + 1) * N_KEYS, :]
        rank = jnp.zeros_like(s)
        for j in range(PEER_TOPK):
            rank = jnp.where(v2_ref[j, h:h + 1, :] > s, float(j + 1), rank)
        r2_out[h * N_KEYS:(h + 1) * N_KEYS, :] = rank.astype(BF16)
        e2_out[h * N_KEYS:(h + 1) * N_KEYS, :] = jnp.exp(s - v2_ref[0, h:h + 1, :]).astype(BF16)


def _peer_route(q, ska0, ska1, skb1, tn):
    n, d = q.shape
    rows = PEER_HEADS * N_KEYS
    full = lambda a: pl.BlockSpec(a.shape, lambda i: (0,) * a.ndim)
    col = pl.BlockSpec((rows, tn), lambda i: (0, i))
    return pl.pallas_call(
        _peer_route_kernel,
        grid=(n // tn,),
        in_specs=[pl.BlockSpec((tn, d), lambda i: (i, 0)), full(ska0), full(ska1), full(skb1)],
        out_specs=[col] * 4,
        out_shape=[jax.ShapeDtypeStruct((rows, n), F32), jax.ShapeDtypeStruct((rows, n), F32),
                   jax.ShapeDtypeStruct((rows, n), BF16), jax.ShapeDtypeStruct((rows, n), BF16)],
        scratch_shapes=[pltpu.VMEM((rows, tn), F32), pltpu.VMEM((rows, tn), F32),
                        pltpu.VMEM((PEER_TOPK, PEER_HEADS, tn), F32)],
        compiler_params=_cparams(("arbitrary",)),
        name="peer_route",
    )(q, ska0, ska1, skb1)


def _peer_dense_kernel(n_eb, ct_ref, u_ref, vt_ref, n_ref, e1_ref, r2_ref, e2_ref, out_ref,
                       ht0_ref, ht1_ref, cf0_ref, cf1_ref, nb_ref, eb_ref):
    s = pl.program_id(0)

    @pl.when(s == 0)
    def _():
        for ref in (ht0_ref, ht1_ref, cf0_ref, cf1_ref):
            ref[...] = jnp.zeros_like(ref)

    @pl.when((s <= 2) | ((s - 2) % n_eb == 0))
    def _():
        out_ref[...] = jnp.zeros_like(out_ref)

    nh = PEER_HEADS
    tn = ct_ref.shape[1]
    rg = BF16_ROWS
    a_blk = u_ref.shape[0] // N_KEYS
    zero = jnp.zeros((rg, tn), BF16)
    g_half = N_KEYS // rg // 2

    def step(ht_new, ht_old, cf_old, cf_new):
        n_chunk = a_blk // 2
        e_rows, d_rows = u_ref.shape[0] // n_chunk, out_ref.shape[0] // n_chunk
        for ch in range(n_chunk):
            rows = slice(ch * e_rows, (ch + 1) * e_rows)
            ht_new[rows, :] = _dot(u_ref[rows, :], ct_ref[...])
            for r in range(2 * ch * nh, (2 * ch + 2) * nh):
                for cols in [slice(lt * LANES, (lt + 1) * LANES) for lt in range(tn // LANES)]:
                    for src, dst in ((n_ref, nb_ref), (e1_ref, eb_ref)):
                        dst[r * rg:(r + 1) * rg, cols] = jnp.broadcast_to(src[r:r + 1, cols], (rg, LANES)).astype(BF16)
            for al, g0 in [(a, g) for a in range(2 * ch, 2 * ch + 2) for g in (0, g_half)]:
                gate = [zero] * g_half
                for h in range(nh):
                    r = al * nh + h
                    n_a, e1_a = nb_ref[r * rg:(r + 1) * rg, :], eb_ref[r * rg:(r + 1) * rg, :]
                    for g in range(g_half):
                        rows = slice(h * N_KEYS + (g0 + g) * rg, h * N_KEYS + (g0 + g + 1) * rg)
                        gate[g] = gate[g] + jnp.where(r2_ref[rows, :] < n_a, e2_ref[rows, :], zero) * e1_a
                for g in range(g_half):
                    rows = slice(al * N_KEYS + (g0 + g) * rg, al * N_KEYS + (g0 + g + 1) * rg)
                    ht = ht_old[rows, :].astype(BF16)
                    act = ht * (0.5 * (1.0 + lax.erf(ht * (2.0 ** -0.5))))
                    cf_old[rows, :] = act * gate[g]
            rows = slice(ch * d_rows, (ch + 1) * d_rows)
            out_ref[rows, :] += _dot(vt_ref[rows, :], cf_new[...])

    @pl.when(s % 2 == 0)
    def _():
        step(ht0_ref, ht1_ref, cf1_ref, cf0_ref)

    @pl.when(s % 2 == 1)
    def _():
        step(ht1_ref, ht0_ref, cf0_ref, cf1_ref)


def _peer_dense(ct, u_bf, vt_bf, n_a, e1, r2, e2, tn, a_blk):
    d, n = ct.shape
    e_blk = a_blk * N_KEYS
    n_eb = u_bf.shape[0] // e_blk
    last = (n // tn) * n_eb - 1
    rows = PEER_HEADS * N_KEYS
    pair = lambda s, lag: jnp.clip(s - lag, 0, last)
    tok = lambda s, lag: pair(s, lag) // n_eb
    blk = lambda s, lag: pair(s, lag) % n_eb
    return pl.pallas_call(
        functools.partial(_peer_dense_kernel, n_eb),
        grid=(last + 3,),
        in_specs=[pl.BlockSpec((d, tn), lambda s: (0, tok(s, 0))),
                  pl.BlockSpec((e_blk, d), lambda s: (blk(s, 0), 0)),
                  pl.BlockSpec((d, e_blk), lambda s: (0, blk(s, 2))),
                  pl.BlockSpec((a_blk * PEER_HEADS, tn), lambda s: (blk(s, 1), tok(s, 1))),
                  pl.BlockSpec((a_blk * PEER_HEADS, tn), lambda s: (blk(s, 1), tok(s, 1))),
                  pl.BlockSpec((rows, tn), lambda s: (0, tok(s, 1))),
                  pl.BlockSpec((rows, tn), lambda s: (0, tok(s, 1)))],
        out_specs=pl.BlockSpec((d, tn), lambda s: (0, tok(s, 2))),
        out_shape=jax.ShapeDtypeStruct((d, n), F32),
        scratch_shapes=([pltpu.VMEM((e_blk, tn), F32)] * 2 + [pltpu.VMEM((e_blk, tn), BF16)] * 2
                        + [pltpu.VMEM((a_blk * PEER_HEADS * BF16_ROWS, tn), BF16)] * 2),
        compiler_params=_cparams(("arbitrary",)),
        name="peer_dense",
    )(ct, u_bf, vt_bf, n_a, e1, r2, e2)


def _final_kernel(last_layer, h1_ref, ffnt_ref, p_ref, png_ref, wg_ref, wp_ref, fng_ref, out_ref):
    h2 = h1_ref[...] + ffnt_ref[...].T
    gate = jax.nn.sigmoid(_dot(_rms(h2, png_ref[...]).astype(BF16), wg_ref[...]))
    h3 = h2 + gate * _dot(p_ref[...].astype(BF16), wp_ref[...])
    out_ref[...] = _rms(h3, fng_ref[...]) if last_layer else h3


def _final(h1, ffnt, p, png, wg, wp, fng, tm, last_layer):
    n, d = h1.shape
    full = lambda a: pl.BlockSpec(a.shape, lambda i: (0,) * a.ndim)
    return pl.pallas_call(
        functools.partial(_final_kernel, last_layer),
        grid=(n // tm,),
        in_specs=[pl.BlockSpec((tm, d), lambda i: (i, 0)), pl.BlockSpec((d, tm), lambda i: (0, i)),
                  pl.BlockSpec((tm, p.shape[1]), lambda i: (i, 0)), full(png), full(wg), full(wp), full(fng)],
        out_specs=pl.BlockSpec((tm, d), lambda i: (i, 0)),
        out_shape=jax.ShapeDtypeStruct((n, d), F32),
        compiler_params=_cparams(("arbitrary",)),
        name="final",
    )(h1, ffnt, p, png, wg, wp, fng)


def _tile(n, want):
    t = min(n, want)
    assert n % t == 0, (n, t)
    return t


def _tiles(seq, n):
    return dict(in_proj=_tile(seq, 512), prep=_tile(seq, 512), scan_steps=_tile(seq, 32), mix=_tile(seq, 512),
                route=_tile(n, 256), dense_tokens=_tile(n, 1024), dense_keys=8, final=_tile(seq, 512))


def kernel(x, p, norm_mix_g, w_in, rwkv_mu, rwkv_w0, rwkv_w_up, rwkv_a0, rwkv_a_up, rwkv_g_up, rwkv_k_k, rwkv_k_a, rwkv_r_k, rwkv_lnx_g, rwkv_lnx_b, rwkv_w_o, conv_dw_w, conv_dw_b, conv_ln_g, conv_ln_b, conv_w_o, w_out, norm_ffn_g, peer_w_q, peer_sub_keys, peer_u, peer_v, ple_w_proj, ple_norm_g, ple_w_gate, final_norm_g):
    bsz, seq, d = x.shape
    n = bsz * seq
    nh, hd = RWKV_HEADS, RWKV_HEAD_DIM
    row = lambda a: a.reshape(1, -1)
    head_of = np.arange(RWKV_WIDTH) // hd
    seg = jnp.asarray(head_of[:, None] == head_of[None, :], BF16)
    h = x.reshape(n, d)
    tiles = _tiles(seq, n)

    for l in range(w_in.shape[0]):
        zr, hg, sg = _in_proj(h, row(norm_mix_g[l]), w_in[l].astype(BF16), tiles["in_proj"])
        r, w, k, v, nkk, kka, g, bonus = _rwkv_prep(
            zr, row(rwkv_mu[l]), row(rwkv_w0[l]), rwkv_w_up[l].astype(BF16), row(rwkv_a0[l]),
            rwkv_a_up[l].astype(BF16), rwkv_g_up[l].astype(BF16),
            row(rwkv_k_k[l]), row(rwkv_k_a[l]), row(rwkv_r_k[l]), seg, tiles["prep"], seq)

        seqs = lambda a: a.reshape(bsz, seq, RWKV_WIDTH)
        y = _rwkv_scan(seqs(r), seqs(w), seqs(k), seqs(v), seqs(kka), seqs(nkk), tiles["scan_steps"])
        y = y.reshape(n, RWKV_WIDTH)

        consts = (row(rwkv_lnx_g[l]), row(rwkv_lnx_b[l]), rwkv_w_o[l].astype(BF16), conv_dw_w[l],
                  row(conv_dw_b[l]), row(conv_ln_g[l]), row(conv_ln_b[l]), conv_w_o[l].astype(BF16),
                  w_out[l].astype(BF16), row(norm_ffn_g[l]),
                  peer_w_q[l].reshape(d, PEER_HEADS, 2, PEER_HALF).transpose(0, 2, 1, 3).reshape(d, -1).astype(BF16))
        h1, ct, q = _mix_out(h, y, g, bonus, hg, sg, seg, consts, tiles["mix"], seq)

        sk = peer_sub_keys[l]
        eye = jnp.eye(PEER_HEADS, dtype=sk.dtype)
        blk = lambda half, order: jnp.einsum("hnd,hg->" + order, sk[:, half], eye).reshape(
            PEER_HEADS * N_KEYS, PEER_HEADS * PEER_HALF).astype(BF16)
        n_a, e1, r2, e2 = _peer_route(q, blk(0, "nhgd"), blk(1, "nhgd"), blk(1, "hngd"), tiles["route"])
        ffnt = _peer_dense(ct, peer_u[l].astype(BF16), peer_v[l].T.astype(BF16), n_a, e1, r2, e2,
                           tiles["dense_tokens"], tiles["dense_keys"])
        h = _final(h1, ffnt, p[l].reshape(n, -1), row(ple_norm_g[l]), ple_w_gate[l].astype(BF16),
                   ple_w_proj[l].astype(BF16), row(final_norm_g), tiles["final"], l == w_in.shape[0] - 1)
    return h.reshape(bsz, seq, d)
```

```python
import functools

import numpy as np
import jax
import jax.numpy as jnp
from jax import lax
from jax.experimental import pallas as pl
from jax.experimental.pallas import tpu as pltpu

F32 = jnp.float32
BF16 = jnp.bfloat16

RWKV_HEADS = 8
RWKV_HEAD_DIM = 64
RWKV_WIDTH = RWKV_HEADS * RWKV_HEAD_DIM
DECAY_LORA = 64
AAA_LORA = 64
GATE_LORA = 128
RWKV_COLS = 3 * RWKV_WIDTH + DECAY_LORA + AAA_LORA + GATE_LORA
GN_EPS = 64e-5
CONV_WIDTH = 512
CONV_K = 31
CONV_HALO = 32
LN_EPS = 1e-5
RMS_EPS = 1e-6
PEER_HEADS = 8
N_KEYS = 128
PEER_HALF = 64
PEER_TOPK = 16

VMEM_LIMIT = 56 * 1024 * 1024
LANES = 128
SUBLANES = 8
BF16_ROWS = 16


def _cparams(sem):
    return pltpu.CompilerParams(dimension_semantics=sem, vmem_limit_bytes=VMEM_LIMIT)


def _dot(a, b):
    return jnp.dot(a, b, preferred_element_type=F32)


def _dot_nt(a, b):
    return lax.dot_general(a, b, (((1,), (1,)), ((), ())), preferred_element_type=F32)


def _dot_split(x, m_bf16):
    hi = x.astype(BF16)
    lo = (x - hi.astype(F32)).astype(BF16)
    return _dot(hi, m_bf16) + _dot(lo, m_bf16)


def _rms(x, g):
    return x * lax.rsqrt(jnp.mean(x * x, axis=-1, keepdims=True) + RMS_EPS) * g


def _in_proj_kernel(x_ref, g_ref, w_ref, zr_ref, hg_ref, sg_ref):
    a = _rms(x_ref[...], g_ref[...]).astype(BF16)
    c0, c1, c2 = RWKV_COLS, RWKV_COLS + CONV_WIDTH, RWKV_COLS + 2 * CONV_WIDTH
    zr_ref[...] = _dot(a, w_ref[:, :c0])
    u = _dot(a, w_ref[:, c0:c1])
    gate = _dot(a, w_ref[:, c1:c2])
    hg_ref[...] = u * jax.nn.sigmoid(gate)
    sg_ref[...] = jax.nn.sigmoid(_dot(a, w_ref[:, c2:])).astype(BF16)


def _in_proj(x, g, w_in, tm):
    n, d = x.shape
    cols = w_in.shape[1]
    gate_cols = cols - RWKV_COLS - 2 * CONV_WIDTH
    row = lambda w: pl.BlockSpec((tm, w), lambda i: (i, 0))
    full = lambda a: pl.BlockSpec(a.shape, lambda i: (0,) * a.ndim)
    return pl.pallas_call(
        _in_proj_kernel,
        grid=(n // tm,),
        in_specs=[row(d), full(g), full(w_in)],
        out_specs=[row(RWKV_COLS), row(CONV_WIDTH), row(gate_cols)],
        out_shape=[jax.ShapeDtypeStruct((n, RWKV_COLS), F32),
                   jax.ShapeDtypeStruct((n, CONV_WIDTH), F32),
                   jax.ShapeDtypeStruct((n, gate_cols), BF16)],
        compiler_params=_cparams(("arbitrary",)),
        name="in_proj",
    )(x, g, w_in)


def _rwkv_prep_kernel(tiles_per_seq, z_ref, mu_ref, w0_ref, wup_ref, a0_ref, aup_ref, gup_ref,
                      kk_ref, ka_ref, rk_ref, seg_ref,
                      r_out, w_out, k_out, v_out, nkk_out, kka_out, g_out, bonus_out, last_ref):
    i = pl.program_id(0)

    @pl.when(i % tiles_per_seq == 0)
    def _():
        last_ref[...] = jnp.zeros_like(last_ref)

    z = z_ref[...]
    tm = z.shape[0]
    rows = lax.broadcasted_iota(jnp.int32, z.shape, 0)
    zs = jnp.where(rows == 0, last_ref[...], pltpu.roll(z, 1, axis=0))
    last_ref[...] = z[tm - 1:tm, :]
    z = z + (zs - z) * mu_ref[...]

    s0 = RWKV_WIDTH
    r = z[:, :s0]
    k = z[:, s0:2 * s0]
    v = z[:, 2 * s0:3 * s0]
    o = 3 * s0
    wl = z[:, o:o + DECAY_LORA]
    al = z[:, o + DECAY_LORA:o + DECAY_LORA + AAA_LORA]
    gl = z[:, o + DECAY_LORA + AAA_LORA:]

    w_log = -jax.nn.softplus(-(w0_ref[...] + _dot_split(jnp.tanh(wl), wup_ref[...]))) - 0.5
    decay = jnp.exp(-jnp.exp(w_log))
    a = jax.nn.sigmoid(a0_ref[...] + _dot_split(al, aup_ref[...]))
    g = _dot_split(jax.nn.sigmoid(gl), gup_ref[...])

    seg = seg_ref[...]
    kk = k * kk_ref[...]
    kk = kk * lax.rsqrt(jnp.maximum(_dot_split(kk * kk, seg), 1e-24))
    k2 = k * (1.0 + (a - 1.0) * ka_ref[...])
    bonus = _dot_split(r * k2 * rk_ref[...], seg) * v

    r_out[...] = r
    w_out[...] = decay
    k_out[...] = k2
    v_out[...] = v
    nkk_out[...] = -kk
    kka_out[...] = kk * a
    g_out[...] = g
    bonus_out[...] = bonus


def _rwkv_prep(zr, mu, w0, w_up, a0, a_up, g_up, k_k, k_a, r_k, seg, tm, seq):
    n = zr.shape[0]
    row = lambda w: pl.BlockSpec((tm, w), lambda i: (i, 0))
    full = lambda a: pl.BlockSpec(a.shape, lambda i: (0,) * a.ndim)
    consts = (mu, w0, w_up, a0, a_up, g_up, k_k, k_a, r_k, seg)
    return pl.pallas_call(
        functools.partial(_rwkv_prep_kernel, seq // tm),
        grid=(n // tm,),
        in_specs=[row(RWKV_COLS)] + [full(c) for c in consts],
        out_specs=[row(RWKV_WIDTH)] * 8,
        out_shape=[jax.ShapeDtypeStruct((n, RWKV_WIDTH), F32)] * 8,
        scratch_shapes=[pltpu.VMEM((1, RWKV_COLS), F32)],
        compiler_params=_cparams(("arbitrary",)),
        name="rwkv_prep",
    )(zr, *consts)


def _rwkv_scan_kernel(r_ref, w_ref, k_ref, v_ref, kka_ref, nkk_ref, y_ref, s_ref, sa_ref, yt_ref, rows_ref, next_ref,
                      pairs_ref):
    @pl.when(pl.program_id(0) == 0)
    def _():
        s_ref[...] = jnp.zeros_like(s_ref)
        yt_ref[...] = jnp.zeros_like(yt_ref)

    bsz, steps, width = r_ref.shape
    nh, hd = RWKV_HEADS, RWKV_HEAD_DIM
    chains = bsz * nh

    def by_pairs(ref, t):
        x = ref[:, t, :]
        return jnp.concatenate([x[:, j * LANES:(j + 1) * LANES] for j in range(width // LANES)], axis=0)

    def chains_on_lanes(pairs):
        xt = pairs.T
        return jnp.concatenate([xt[:hd, :], xt[hd:, :]], axis=1)

    to_lanes = lambda ref, t: chains_on_lanes(by_pairs(ref, t))

    w_i, k_i, kka_i, r_i, nkk_i, v_i = range(6)

    def gather(t):
        t = jnp.minimum(t, steps - 1)
        for i, ref in ((w_i, w_ref), (k_i, k_ref), (kka_i, kka_ref), (r_i, r_ref), (v_i, v_ref)):
            pairs_ref[i] = by_pairs(ref, t)
        pairs_ref[nkk_i] = by_pairs(nkk_ref, jnp.minimum(t + 1, steps - 1))

    def flip():
        for i in range(6):
            next_ref[i] = chains_on_lanes(pairs_ref[i])

    def emit(t):
        y = yt_ref[...]
        yt = jnp.concatenate([y[:, :chains // 2], y[:, chains // 2:]], axis=0).T
        y_ref[:, t, :] = jnp.concatenate([yt[j * bsz:(j + 1) * bsz, :] for j in range(width // LANES)], axis=1)

    rows_ref[nkk_i] = to_lanes(nkk_ref, 0)
    part = [jnp.zeros((hd, chains), F32) for _ in range(4)]
    for kx in range(hd):
        part[kx % 4] = part[kx % 4] + s_ref[kx] * rows_ref[nkk_i, pl.ds(kx, 1), :]
    sa_ref[...] = (part[0] + part[1]) + (part[2] + part[3])
    gather(0)
    flip()
    rows_ref[...] = next_ref[...]
    gather(1)

    def step(t, carry):
        flip()
        row = lambda i, kx: rows_ref[i, pl.ds(kx, 1), :]
        v_t, sa_old = rows_ref[v_i], sa_ref[...]
        y = [jnp.zeros_like(v_t), jnp.zeros_like(v_t)]
        sa = [jnp.zeros_like(v_t), jnp.zeros_like(v_t)]
        for kx in range(hd):
            s_k = s_ref[kx] * row(w_i, kx) + sa_old * row(kka_i, kx) + v_t * row(k_i, kx)
            s_ref[kx] = s_k
            y[kx % 2] = y[kx % 2] + s_k * row(r_i, kx)
            sa[kx % 2] = sa[kx % 2] + s_k * row(nkk_i, kx)
        sa_ref[...] = sa[0] + sa[1]
        emit(jnp.maximum(t - 1, 0))
        yt_ref[...] = y[0] + y[1]
        gather(t + 2)
        rows_ref[...] = next_ref[...]
        return carry

    lax.fori_loop(0, steps, step, 0)
    emit(steps - 1)


def _rwkv_scan(r, w, k, v, kka, nkk, steps):
    bsz, t, width = r.shape
    nh, hd = RWKV_HEADS, RWKV_HEAD_DIM
    blk = pl.BlockSpec((bsz, steps, width), lambda i: (0, i, 0))
    return pl.pallas_call(
        _rwkv_scan_kernel,
        grid=(t // steps,),
        in_specs=[blk] * 6,
        out_specs=blk,
        out_shape=jax.ShapeDtypeStruct((bsz, t, width), F32),
        scratch_shapes=[pltpu.VMEM((hd, hd, bsz * nh), F32), pltpu.VMEM((hd, bsz * nh), F32),
                        pltpu.VMEM((hd, bsz * nh), F32), pltpu.VMEM((6, hd, bsz * nh), F32),
                        pltpu.VMEM((6, hd, bsz * nh), F32), pltpu.VMEM((6, bsz * width // LANES, LANES), F32)],
        compiler_params=_cparams(("arbitrary",)),
        name="rwkv_scan",
    )(r, w, k, v, kka, nkk)


def _mix_out_kernel(tiles_per_seq, x_ref, y_ref, g_ref, bonus_ref, hg_ref, sg_ref, seg_ref,
                    lnxg_ref, lnxb_ref, wor_ref, dww_ref, dwb_ref, clng_ref, clnb_ref, woc_ref,
                    wout_ref, nfg_ref, wq_ref, h1_out, ct_out, q_out, buf_ref):
    i = pl.program_id(0)
    tm = x_ref.shape[0]

    @pl.when(i % tiles_per_seq == 0)
    def _():
        buf_ref[:CONV_HALO, :] = jnp.zeros((CONV_HALO, CONV_WIDTH), F32)
        buf_ref[CONV_HALO + tm:, :] = jnp.zeros((SUBLANES, CONV_WIDTH), F32)

    seg = seg_ref[...]
    y = y_ref[...]
    inv_n = 1.0 / RWKV_HEAD_DIM
    mean = _dot_split(y, seg) * inv_n
    yc = y - mean
    var = _dot_split(yc * yc, seg) * inv_n
    yn = yc * lax.rsqrt(var + GN_EPS) * lnxg_ref[...] + lnxb_ref[...]
    o_r = _dot(((yn + bonus_ref[...]) * g_ref[...]).astype(BF16), wor_ref[...])

    buf_ref[CONV_HALO:CONV_HALO + tm, :] = hg_ref[...]
    acc = jnp.zeros((tm, CONV_WIDTH), F32) + dwb_ref[...]
    for phase in range(SUBLANES):
        part = None
        for j in range(CONV_K):
            off = CONV_HALO - (CONV_K - 1) + j
            if off % SUBLANES == phase:
                base = off - phase
                term = buf_ref[base:base + tm + SUBLANES, :] * dww_ref[j:j + 1, :]
                part = term if part is None else part + term
        acc = acc + part[phase:phase + tm, :]
    buf_ref[:CONV_HALO, :] = buf_ref[tm:tm + CONV_HALO, :]
    mu = jnp.mean(acc, axis=-1, keepdims=True)
    ac = acc - mu
    cv = jnp.mean(ac * ac, axis=-1, keepdims=True)
    hf = ac * lax.rsqrt(cv + LN_EPS) * clng_ref[...] + clnb_ref[...]
    o_c = _dot((hf * jax.nn.sigmoid(hf)).astype(BF16), woc_ref[...])

    d = x_ref.shape[1]
    sg = sg_ref[...].astype(F32)
    mix = sg[:, :d] * o_r + sg[:, d:] * o_c
    h1 = x_ref[...] + _dot(mix.astype(BF16), wout_ref[...])
    h1_out[...] = h1
    c = _rms(h1, nfg_ref[...])
    ct_out[...] = c.T.astype(BF16)
    q_out[...] = _dot(c.astype(BF16), wq_ref[...])


def _mix_out(x, y, g, bonus, hg, sg, seg, consts, tm, seq):
    n, d = x.shape
    row = lambda w: pl.BlockSpec((tm, w), lambda i: (i, 0))
    full = lambda a: pl.BlockSpec(a.shape, lambda i: (0,) * a.ndim)
    return pl.pallas_call(
        functools.partial(_mix_out_kernel, seq // tm),
        grid=(n // tm,),
        in_specs=[row(d), row(RWKV_WIDTH), row(RWKV_WIDTH), row(RWKV_WIDTH), row(CONV_WIDTH), row(2 * d),
                  full(seg)] + [full(c) for c in consts],
        out_specs=[row(d), pl.BlockSpec((d, tm), lambda i: (0, i)), row(d)],
        out_shape=[jax.ShapeDtypeStruct((n, d), F32), jax.ShapeDtypeStruct((d, n), BF16),
                   jax.ShapeDtypeStruct((n, d), F32)],
        scratch_shapes=[pltpu.VMEM((tm + CONV_HALO + SUBLANES, CONV_WIDTH), F32)],
        compiler_params=_cparams(("arbitrary",)),
        name="mix_out",
    )(x, y, g, bonus, hg, sg, seg, *consts)


def _sort16_pairs():
    n, pairs, p = PEER_TOPK, [], 1
    while p < n:
        k = p
        while k >= 1:
            for j in range(k % p, n - k, 2 * k):
                for i in range(min(k, n - j - k)):
                    if (i + j) // (2 * p) == (i + j + k) // (2 * p):
                        pairs.append((i + j, i + j + k))
            k //= 2
        p *= 2
    return pairs


_SORT16 = _sort16_pairs()
_BITONIC16 = [(i, i + dd) for dd in (8, 4, 2, 1) for i in range(PEER_TOPK) if not i & dd]
_CELLS = [(i, j) for i in range(PEER_TOPK) for j in range(PEER_TOPK) if (i + 1) * (j + 1) <= PEER_TOPK]


def _exchange(vals, pairs):
    for a, b in pairs:
        hi, lo = jnp.maximum(vals[a], vals[b]), jnp.minimum(vals[a], vals[b])
        vals[a], vals[b] = hi, lo
    return vals


def _top16_sorted(slabs):
    groups = []
    for s in range(0, len(slabs), PEER_TOPK):
        groups.append(_exchange(list(slabs[s:s + PEER_TOPK]), _SORT16))
    while len(groups) > 1:
        merged = []
        for a, b in zip(groups[::2], groups[1::2]):
            c = [jnp.maximum(a[i], b[PEER_TOPK - 1 - i]) for i in range(PEER_TOPK)]
            merged.append(_exchange(c, _BITONIC16))
        groups = merged
    return groups[0]


def _peer_route_kernel(q_ref, ska0_ref, ska1_ref, skb1_ref, n_out, e1_out, r2_out, e2_out, s1_ref, s2_ref, v2_ref):
    qh = q_ref[...].astype(BF16)
    half = qh.shape[1] // 2
    s1_ref[...] = _dot_nt(ska0_ref[...], qh[:, :half])
    s2_ref[...] = _dot_nt(ska1_ref[...], qh[:, half:])
    nh = PEER_HEADS
    slab = lambda ref, n: ref[n * nh:(n + 1) * nh, :]

    v1 = _top16_sorted([slab(s1_ref, n) for n in range(N_KEYS)])
    v2 = _top16_sorted([slab(s2_ref, n) for n in range(N_KEYS)])

    cand = {c: v1[c[0]] + v2[c[1]] for c in _CELLS}
    cnt = {c: jnp.full_like(v1[0], float((c[0] + 1) * (c[1] + 1) - 1)) for c in _CELLS}
    for c1 in _CELLS:
        for c2 in _CELLS:
            if c1[0] < c2[0] and c1[1] > c2[1]:
                first = cand[c1] >= cand[c2]
                cnt[c2] = cnt[c2] + jnp.where(first, 1.0, 0.0)
                cnt[c1] = cnt[c1] + jnp.where(first, 0.0, 1.0)
    top = cand[(0, 0)]
    z = jnp.zeros_like(top)
    taken = [jnp.zeros_like(top) for _ in range(PEER_TOPK)]
    for c in _CELLS:
        sel = cnt[c] < float(PEER_TOPK)
        z = z + jnp.where(sel, jnp.exp(cand[c] - top), 0.0)
        taken[c[0]] = taken[c[0]] + jnp.where(sel, 1.0, 0.0)
    inv_z = 1.0 / z

    for n in range(N_KEYS):
        s = slab(s1_ref, n)
        cnt_a = jnp.zeros_like(s)
        for i in reversed(range(PEER_TOPK)):
            cnt_a = jnp.where(s == v1[i], taken[i], cnt_a)
        n_out[n * nh:(n + 1) * nh, :] = cnt_a
        e1_out[n * nh:(n + 1) * nh, :] = jnp.exp(s - v1[0]) * inv_z

    for j in range(PEER_TOPK):
        v2_ref[j] = v2[j]
    s2b = _dot_nt(skb1_ref[...], qh[:, half:])
    for h in range(nh):
        s = s2b[h * N_KEYS:(h + 1) * N_KEYS, :]
        rank = jnp.zeros_like(s)
        for j in range(PEER_TOPK):
            rank = jnp.where(v2_ref[j, h:h + 1, :] > s, float(j + 1), rank)
        r2_out[h * N_KEYS:(h + 1) * N_KEYS, :] = rank.astype(BF16)
        e2_out[h * N_KEYS:(h + 1) * N_KEYS, :] = jnp.exp(s - v2_ref[0, h:h + 1, :]).astype(BF16)


def _peer_route(q, ska0, ska1, skb1, tn):
    n, d = q.shape
    rows = PEER_HEADS * N_KEYS
    full = lambda a: pl.BlockSpec(a.shape, lambda i: (0,) * a.ndim)
    col = pl.BlockSpec((rows, tn), lambda i: (0, i))
    return pl.pallas_call(
        _peer_route_kernel,
        grid=(n // tn,),
        in_specs=[pl.BlockSpec((tn, d), lambda i: (i, 0)), full(ska0), full(ska1), full(skb1)],
        out_specs=[col] * 4,
        out_shape=[jax.ShapeDtypeStruct((rows, n), F32), jax.ShapeDtypeStruct((rows, n), F32),
                   jax.ShapeDtypeStruct((rows, n), BF16), jax.ShapeDtypeStruct((rows, n), BF16)],
        scratch_shapes=[pltpu.VMEM((rows, tn), F32), pltpu.VMEM((rows, tn), F32),
                        pltpu.VMEM((PEER_TOPK, PEER_HEADS, tn), F32)],
        compiler_params=_cparams(("arbitrary",)),
        name="peer_route",
    )(q, ska0, ska1, skb1)


def _peer_dense_kernel(n_eb, ct_ref, u_ref, vt_ref, n_ref, e1_ref, r2_ref, e2_ref, out_ref,
                       ht0_ref, ht1_ref, cf0_ref, cf1_ref, nb_ref, eb_ref):
    s = pl.program_id(0)

    @pl.when(s == 0)
    def _():
        for ref in (ht0_ref, ht1_ref, cf0_ref, cf1_ref):
            ref[...] = jnp.zeros_like(ref)

    @pl.when((s <= 2) | ((s - 2) % n_eb == 0))
    def _():
        out_ref[...] = jnp.zeros_like(out_ref)

    nh = PEER_HEADS
    tn = ct_ref.shape[1]
    rg = BF16_ROWS
    a_blk = u_ref.shape[0] // N_KEYS
    zero = jnp.zeros((rg, tn), BF16)
    g_half = N_KEYS // rg // 2

    def step(ht_new, ht_old, cf_old, cf_new):
        n_chunk = a_blk // 2
        e_rows, d_rows = u_ref.shape[0] // n_chunk, out_ref.shape[0] // n_chunk
        for ch in range(n_chunk):
            rows = slice(ch * e_rows, (ch + 1) * e_rows)
            ht_new[rows, :] = _dot(u_ref[rows, :], ct_ref[...])
            for r in range(2 * ch * nh, (2 * ch + 2) * nh):
                for cols in [slice(lt * LANES, (lt + 1) * LANES) for lt in range(tn // LANES)]:
                    for src, dst in ((n_ref, nb_ref), (e1_ref, eb_ref)):
                        dst[r * rg:(r + 1) * rg, cols] = jnp.broadcast_to(src[r:r + 1, cols], (rg, LANES)).astype(BF16)
            for al, g0 in [(a, g) for a in range(2 * ch, 2 * ch + 2) for g in (0, g_half)]:
                gate = [zero] * g_half
                for h in range(nh):
                    r = al * nh + h
                    n_a, e1_a = nb_ref[r * rg:(r + 1) * rg, :], eb_ref[r * rg:(r + 1) * rg, :]
                    for g in range(g_half):
                        rows = slice(h * N_KEYS + (g0 + g) * rg, h * N_KEYS + (g0 + g + 1) * rg)
                        gate[g] = gate[g] + jnp.where(r2_ref[rows, :] < n_a, e2_ref[rows, :], zero) * e1_a
                for g in range(g_half):
                    rows = slice(al * N_KEYS + (g0 + g) * rg, al * N_KEYS + (g0 + g + 1) * rg)
                    ht = ht_old[rows, :].astype(BF16)
                    act = ht * (0.5 * (1.0 + lax.erf(ht * (2.0 ** -0.5))))
                    cf_old[rows, :] = act * gate[g]
            rows = slice(ch * d_rows, (ch + 1) * d_rows)
            out_ref[rows, :] += _dot(vt_ref[rows, :], cf_new[...])

    @pl.when(s % 2 == 0)
    def _():
        step(ht0_ref, ht1_ref, cf1_ref, cf0_ref)

    @pl.when(s % 2 == 1)
    def _():
        step(ht1_ref, ht0_ref, cf0_ref, cf1_ref)


def _peer_dense(ct, u_bf, vt_bf, n_a, e1, r2, e2, tn, a_blk):
    d, n = ct.shape
    e_blk = a_blk * N_KEYS
    n_eb = u_bf.shape[0] // e_blk
    last = (n // tn) * n_eb - 1
    rows = PEER_HEADS * N_KEYS
    pair = lambda s, lag: jnp.clip(s - lag, 0, last)
    tok = lambda s, lag: pair(s, lag) // n_eb
    blk = lambda s, lag: pair(s, lag) % n_eb
    return pl.pallas_call(
        functools.partial(_peer_dense_kernel, n_eb),
        grid=(last + 3,),
        in_specs=[pl.BlockSpec((d, tn), lambda s: (0, tok(s, 0))),
                  pl.BlockSpec((e_blk, d), lambda s: (blk(s, 0), 0)),
                  pl.BlockSpec((d, e_blk), lambda s: (0, blk(s, 2))),
                  pl.BlockSpec((a_blk * PEER_HEADS, tn), lambda s: (blk(s, 1), tok(s, 1))),
                  pl.BlockSpec((a_blk * PEER_HEADS, tn), lambda s: (blk(s, 1), tok(s, 1))),
                  pl.BlockSpec((rows, tn), lambda s: (0, tok(s, 1))),
                  pl.BlockSpec((rows, tn), lambda s: (0, tok(s, 1)))],
        out_specs=pl.BlockSpec((d, tn), lambda s: (0, tok(s, 2))),
        out_shape=jax.ShapeDtypeStruct((d, n), F32),
        scratch_shapes=([pltpu.VMEM((e_blk, tn), F32)] * 2 + [pltpu.VMEM((e_blk, tn), BF16)] * 2
                        + [pltpu.VMEM((a_blk * PEER_HEADS * BF16_ROWS, tn), BF16)] * 2),
        compiler_params=_cparams(("arbitrary",)),
        name="peer_dense",
    )(ct, u_bf, vt_bf, n_a, e1, r2, e2)


def _final_kernel(last_layer, h1_ref, ffnt_ref, p_ref, png_ref, wg_ref, wp_ref, fng_ref, out_ref):
    h2 = h1_ref[...] + ffnt_ref[...].T
    gate = jax.nn.sigmoid(_dot(_rms(h2, png_ref[...]).astype(BF16), wg_ref[...]))
    h3 = h2 + gate * _dot(p_ref[...].astype(BF16), wp_ref[...])
    out_ref[...] = _rms(h3, fng_ref[...]) if last_layer else h3


def _final(h1, ffnt, p, png, wg, wp, fng, tm, last_layer):
    n, d = h1.shape
    full = lambda a: pl.BlockSpec(a.shape, lambda i: (0,) * a.ndim)
    return pl.pallas_call(
        functools.partial(_final_kernel, last_layer),
        grid=(n // tm,),
        in_specs=[pl.BlockSpec((tm, d), lambda i: (i, 0)), pl.BlockSpec((d, tm), lambda i: (0, i)),
                  pl.BlockSpec((tm, p.shape[1]), lambda i: (i, 0)), full(png), full(wg), full(wp), full(fng)],
        out_specs=pl.BlockSpec((tm, d), lambda i: (i, 0)),
        out_shape=jax.ShapeDtypeStruct((n, d), F32),
        compiler_params=_cparams(("arbitrary",)),
        name="final",
    )(h1, ffnt, p, png, wg, wp, fng)


def _tile(n, want):
    t = min(n, want)
    assert n % t == 0, (n, t)
    return t


def _tiles(seq, n):
    return dict(in_proj=_tile(seq, 512), prep=_tile(seq, 512), scan_steps=_tile(seq, 64), mix=_tile(seq, 512),
                route=_tile(n, 512), dense_tokens=_tile(n, 1024), dense_keys=8, final=_tile(seq, 1024))


def kernel(x, p, norm_mix_g, w_in, rwkv_mu, rwkv_w0, rwkv_w_up, rwkv_a0, rwkv_a_up, rwkv_g_up, rwkv_k_k, rwkv_k_a, rwkv_r_k, rwkv_lnx_g, rwkv_lnx_b, rwkv_w_o, conv_dw_w, conv_dw_b, conv_ln_g, conv_ln_b, conv_w_o, w_out, norm_ffn_g, peer_w_q, peer_sub_keys, peer_u, peer_v, ple_w_proj, ple_norm_g, ple_w_gate, final_norm_g):
    bsz, seq, d = x.shape
    n = bsz * seq
    nh, hd = RWKV_HEADS, RWKV_HEAD_DIM
    row = lambda a: a.reshape(1, -1)
    head_of = np.arange(RWKV_WIDTH) // hd
    seg = jnp.asarray(head_of[:, None] == head_of[None, :], BF16)
    h = x.reshape(n, d)
    tiles = _tiles(seq, n)

    for l in range(w_in.shape[0]):
        zr, hg, sg = _in_proj(h, row(norm_mix_g[l]), w_in[l].astype(BF16), tiles["in_proj"])
        r, w, k, v, nkk, kka, g, bonus = _rwkv_prep(
            zr, row(rwkv_mu[l]), row(rwkv_w0[l]), rwkv_w_up[l].astype(BF16), row(rwkv_a0[l]),
            rwkv_a_up[l].astype(BF16), rwkv_g_up[l].astype(BF16),
            row(rwkv_k_k[l]), row(rwkv_k_a[l]), row(rwkv_r_k[l]), seg, tiles["prep"], seq)

        seqs = lambda a: a.reshape(bsz, seq, RWKV_WIDTH)
        y = _rwkv_scan(seqs(r), seqs(w), seqs(k), seqs(v), seqs(kka), seqs(nkk), tiles["scan_steps"])
        y = y.reshape(n, RWKV_WIDTH)

        consts = (row(rwkv_lnx_g[l]), row(rwkv_lnx_b[l]), rwkv_w_o[l].astype(BF16), conv_dw_w[l],
                  row(conv_dw_b[l]), row(conv_ln_g[l]), row(conv_ln_b[l]), conv_w_o[l].astype(BF16),
                  w_out[l].astype(BF16), row(norm_ffn_g[l]),
                  peer_w_q[l].reshape(d, PEER_HEADS, 2, PEER_HALF).transpose(0, 2, 1, 3).reshape(d, -1).astype(BF16))
        h1, ct, q = _mix_out(h, y, g, bonus, hg, sg, seg, consts, tiles["mix"], seq)

        sk = peer_sub_keys[l]
        eye = jnp.eye(PEER_HEADS, dtype=sk.dtype)
        blk = lambda half, order: jnp.einsum("hnd,hg->" + order, sk[:, half], eye).reshape(
            PEER_HEADS * N_KEYS, PEER_HEADS * PEER_HALF).astype(BF16)
        n_a, e1, r2, e2 = _peer_route(q, blk(0, "nhgd"), blk(1, "nhgd"), blk(1, "hngd"), tiles["route"])
        ffnt = _peer_dense(ct, peer_u[l].astype(BF16), peer_v[l].T.astype(BF16), n_a, e1, r2, e2,
                           tiles["dense_tokens"], tiles["dense_keys"])
        h = _final(h1, ffnt, p[l].reshape(n, -1), row(ple_norm_g[l]), ple_w_gate[l].astype(BF16),
                   ple_w_proj[l].astype(BF16), row(final_norm_g), tiles["final"], l == w_in.shape[0] - 1)
    return h.reshape(bsz, seq, d)
```

```python
import functools

import numpy as np
import jax
import jax.numpy as jnp
from jax import lax
from jax.experimental import pallas as pl
from jax.experimental.pallas import tpu as pltpu

F32 = jnp.float32
BF16 = jnp.bfloat16

RWKV_HEADS = 8
RWKV_HEAD_DIM = 64
RWKV_WIDTH = RWKV_HEADS * RWKV_HEAD_DIM
DECAY_LORA = 64
AAA_LORA = 64
GATE_LORA = 128
RWKV_COLS = 3 * RWKV_WIDTH + DECAY_LORA + AAA_LORA + GATE_LORA
GN_EPS = 64e-5
CONV_WIDTH = 512
CONV_K = 31
CONV_HALO = 32
LN_EPS = 1e-5
RMS_EPS = 1e-6
PEER_HEADS = 8
N_KEYS = 128
PEER_HALF = 64
PEER_TOPK = 16

VMEM_LIMIT = 56 * 1024 * 1024
LANES = 128
SUBLANES = 8
BF16_ROWS = 16


def _cparams(sem):
    return pltpu.CompilerParams(dimension_semantics=sem, vmem_limit_bytes=VMEM_LIMIT)


def _dot(a, b):
    return jnp.dot(a, b, preferred_element_type=F32)


def _dot_nt(a, b):
    return lax.dot_general(a, b, (((1,), (1,)), ((), ())), preferred_element_type=F32)


def _dot_split(x, m_bf16):
    hi = x.astype(BF16)
    lo = (x - hi.astype(F32)).astype(BF16)
    return _dot(hi, m_bf16) + _dot(lo, m_bf16)


def _rms(x, g):
    return x * lax.rsqrt(jnp.mean(x * x, axis=-1, keepdims=True) + RMS_EPS) * g


def _in_proj_kernel(x_ref, g_ref, w_ref, zr_ref, hg_ref, sg_ref):
    a = _rms(x_ref[...], g_ref[...]).astype(BF16)
    c0, c1, c2 = RWKV_COLS, RWKV_COLS + CONV_WIDTH, RWKV_COLS + 2 * CONV_WIDTH
    zr_ref[...] = _dot(a, w_ref[:, :c0])
    u = _dot(a, w_ref[:, c0:c1])
    gate = _dot(a, w_ref[:, c1:c2])
    hg_ref[...] = u * jax.nn.sigmoid(gate)
    sg_ref[...] = jax.nn.sigmoid(_dot(a, w_ref[:, c2:])).astype(BF16)


def _in_proj(x, g, w_in, tm):
    n, d = x.shape
    cols = w_in.shape[1]
    gate_cols = cols - RWKV_COLS - 2 * CONV_WIDTH
    row = lambda w: pl.BlockSpec((tm, w), lambda i: (i, 0))
    full = lambda a: pl.BlockSpec(a.shape, lambda i: (0,) * a.ndim)
    return pl.pallas_call(
        _in_proj_kernel,
        grid=(n // tm,),
        in_specs=[row(d), full(g), full(w_in)],
        out_specs=[row(RWKV_COLS), row(CONV_WIDTH), row(gate_cols)],
        out_shape=[jax.ShapeDtypeStruct((n, RWKV_COLS), F32),
                   jax.ShapeDtypeStruct((n, CONV_WIDTH), F32),
                   jax.ShapeDtypeStruct((n, gate_cols), BF16)],
        compiler_params=_cparams(("arbitrary",)),
        name="in_proj",
    )(x, g, w_in)


def _rwkv_prep_kernel(tiles_per_seq, z_ref, mu_ref, w0_ref, wup_ref, a0_ref, aup_ref, gup_ref,
                      kk_ref, ka_ref, rk_ref, seg_ref,
                      r_out, w_out, k_out, v_out, nkk_out, kka_out, g_out, bonus_out, last_ref):
    i = pl.program_id(0)

    @pl.when(i % tiles_per_seq == 0)
    def _():
        last_ref[...] = jnp.zeros_like(last_ref)

    z = z_ref[...]
    tm = z.shape[0]
    rows = lax.broadcasted_iota(jnp.int32, z.shape, 0)
    zs = jnp.where(rows == 0, last_ref[...], pltpu.roll(z, 1, axis=0))
    last_ref[...] = z[tm - 1:tm, :]
    z = z + (zs - z) * mu_ref[...]

    s0 = RWKV_WIDTH
    r = z[:, :s0]
    k = z[:, s0:2 * s0]
    v = z[:, 2 * s0:3 * s0]
    o = 3 * s0
    wl = z[:, o:o + DECAY_LORA]
    al = z[:, o + DECAY_LORA:o + DECAY_LORA + AAA_LORA]
    gl = z[:, o + DECAY_LORA + AAA_LORA:]

    w_log = -jax.nn.softplus(-(w0_ref[...] + _dot_split(jnp.tanh(wl), wup_ref[...]))) - 0.5
    decay = jnp.exp(-jnp.exp(w_log))
    a = jax.nn.sigmoid(a0_ref[...] + _dot_split(al, aup_ref[...]))
    g = _dot_split(jax.nn.sigmoid(gl), gup_ref[...])

    seg = seg_ref[...]
    kk = k * kk_ref[...]
    kk = kk * lax.rsqrt(jnp.maximum(_dot_split(kk * kk, seg), 1e-24))
    k2 = k * (1.0 + (a - 1.0) * ka_ref[...])
    bonus = _dot_split(r * k2 * rk_ref[...], seg) * v

    r_out[...] = r
    w_out[...] = decay
    k_out[...] = k2
    v_out[...] = v
    nkk_out[...] = -kk
    kka_out[...] = kk * a
    g_out[...] = g
    bonus_out[...] = bonus


def _rwkv_prep(zr, mu, w0, w_up, a0, a_up, g_up, k_k, k_a, r_k, seg, tm, seq):
    n = zr.shape[0]
    row = lambda w: pl.BlockSpec((tm, w), lambda i: (i, 0))
    full = lambda a: pl.BlockSpec(a.shape, lambda i: (0,) * a.ndim)
    consts = (mu, w0, w_up, a0, a_up, g_up, k_k, k_a, r_k, seg)
    return pl.pallas_call(
        functools.partial(_rwkv_prep_kernel, seq // tm),
        grid=(n // tm,),
        in_specs=[row(RWKV_COLS)] + [full(c) for c in consts],
        out_specs=[row(RWKV_WIDTH)] * 8,
        out_shape=[jax.ShapeDtypeStruct((n, RWKV_WIDTH), F32)] * 8,
        scratch_shapes=[pltpu.VMEM((1, RWKV_COLS), F32)],
        compiler_params=_cparams(("arbitrary",)),
        name="rwkv_prep",
    )(zr, *consts)


def _rwkv_scan_kernel(r_ref, w_ref, k_ref, v_ref, kka_ref, nkk_ref, y_ref, s_ref, sa_ref, yt_ref, rows_ref, next_ref,
                      pairs_ref):
    @pl.when(pl.program_id(0) == 0)
    def _():
        s_ref[...] = jnp.zeros_like(s_ref)
        yt_ref[...] = jnp.zeros_like(yt_ref)

    bsz, steps, width = r_ref.shape
    nh, hd = RWKV_HEADS, RWKV_HEAD_DIM
    chains = bsz * nh

    def by_pairs(ref, t):
        x = ref[:, t, :]
        return jnp.concatenate([x[:, j * LANES:(j + 1) * LANES] for j in range(width // LANES)], axis=0)

    def chains_on_lanes(pairs):
        xt = pairs.T
        return jnp.concatenate([xt[:hd, :], xt[hd:, :]], axis=1)

    to_lanes = lambda ref, t: chains_on_lanes(by_pairs(ref, t))

    w_i, k_i, kka_i, r_i, nkk_i, v_i = range(6)

    def gather(t):
        t = jnp.minimum(t, steps - 1)
        for i, ref in ((w_i, w_ref), (k_i, k_ref), (kka_i, kka_ref), (r_i, r_ref), (v_i, v_ref)):
            pairs_ref[i] = by_pairs(ref, t)
        pairs_ref[nkk_i] = by_pairs(nkk_ref, jnp.minimum(t + 1, steps - 1))

    def flip():
        for i in range(6):
            next_ref[i] = chains_on_lanes(pairs_ref[i])

    def emit(t):
        y = yt_ref[...]
        yt = jnp.concatenate([y[:, :chains // 2], y[:, chains // 2:]], axis=0).T
        y_ref[:, t, :] = jnp.concatenate([yt[j * bsz:(j + 1) * bsz, :] for j in range(width // LANES)], axis=1)

    rows_ref[nkk_i] = to_lanes(nkk_ref, 0)
    part = [jnp.zeros((hd, chains), F32) for _ in range(4)]
    for kx in range(hd):
        part[kx % 4] = part[kx % 4] + s_ref[kx] * rows_ref[nkk_i, pl.ds(kx, 1), :]
    sa_ref[...] = (part[0] + part[1]) + (part[2] + part[3])
    gather(0)
    flip()
    rows_ref[...] = next_ref[...]
    gather(1)

    def step(t, carry):
        flip()
        row = lambda i, kx: rows_ref[i, pl.ds(kx, 1), :]
        v_t, sa_old = rows_ref[v_i], sa_ref[...]
        y = [jnp.zeros_like(v_t), jnp.zeros_like(v_t)]
        sa = [jnp.zeros_like(v_t), jnp.zeros_like(v_t)]
        for kx in range(hd):
            s_k = s_ref[kx] * row(w_i, kx) + sa_old * row(kka_i, kx) + v_t * row(k_i, kx)
            s_ref[kx] = s_k
            y[kx % 2] = y[kx % 2] + s_k * row(r_i, kx)
            sa[kx % 2] = sa[kx % 2] + s_k * row(nkk_i, kx)
        sa_ref[...] = sa[0] + sa[1]
        emit(jnp.maximum(t - 1, 0))
        yt_ref[...] = y[0] + y[1]
        gather(t + 2)
        rows_ref[...] = next_ref[...]
        return carry

    lax.fori_loop(0, steps, step, 0)
    emit(steps - 1)


def _rwkv_scan(r, w, k, v, kka, nkk, steps):
    bsz, t, width = r.shape
    nh, hd = RWKV_HEADS, RWKV_HEAD_DIM
    blk = pl.BlockSpec((bsz, steps, width), lambda i: (0, i, 0))
    return pl.pallas_call(
        _rwkv_scan_kernel,
        grid=(t // steps,),
        in_specs=[blk] * 6,
        out_specs=blk,
        out_shape=jax.ShapeDtypeStruct((bsz, t, width), F32),
        scratch_shapes=[pltpu.VMEM((hd, hd, bsz * nh), F32), pltpu.VMEM((hd, bsz * nh), F32),
                        pltpu.VMEM((hd, bsz * nh), F32), pltpu.VMEM((6, hd, bsz * nh), F32),
                        pltpu.VMEM((6, hd, bsz * nh), F32), pltpu.VMEM((6, bsz * width // LANES, LANES), F32)],
        compiler_params=_cparams(("arbitrary",)),
        name="rwkv_scan",
    )(r, w, k, v, kka, nkk)


def _mix_out_kernel(tiles_per_seq, x_ref, y_ref, g_ref, bonus_ref, hg_ref, sg_ref, seg_ref,
                    lnxg_ref, lnxb_ref, wor_ref, dww_ref, dwb_ref, clng_ref, clnb_ref, woc_ref,
                    wout_ref, nfg_ref, wq_ref, h1_out, ct_out, q_out, buf_ref):
    i = pl.program_id(0)
    tm = x_ref.shape[0]

    @pl.when(i % tiles_per_seq == 0)
    def _():
        buf_ref[:CONV_HALO, :] = jnp.zeros((CONV_HALO, CONV_WIDTH), F32)
        buf_ref[CONV_HALO + tm:, :] = jnp.zeros((SUBLANES, CONV_WIDTH), F32)

    seg = seg_ref[...]
    y = y_ref[...]
    inv_n = 1.0 / RWKV_HEAD_DIM
    mean = _dot_split(y, seg) * inv_n
    yc = y - mean
    var = _dot_split(yc * yc, seg) * inv_n
    yn = yc * lax.rsqrt(var + GN_EPS) * lnxg_ref[...] + lnxb_ref[...]
    o_r = _dot(((yn + bonus_ref[...]) * g_ref[...]).astype(BF16), wor_ref[...])

    buf_ref[CONV_HALO:CONV_HALO + tm, :] = hg_ref[...]
    acc = jnp.zeros((tm, CONV_WIDTH), F32) + dwb_ref[...]
    for phase in range(SUBLANES):
        part = None
        for j in range(CONV_K):
            off = CONV_HALO - (CONV_K - 1) + j
            if off % SUBLANES == phase:
                base = off - phase
                term = buf_ref[base:base + tm + SUBLANES, :] * dww_ref[j:j + 1, :]
                part = term if part is None else part + term
        acc = acc + part[phase:phase + tm, :]
    buf_ref[:CONV_HALO, :] = buf_ref[tm:tm + CONV_HALO, :]
    mu = jnp.mean(acc, axis=-1, keepdims=True)
    ac = acc - mu
    cv = jnp.mean(ac * ac, axis=-1, keepdims=True)
    hf = ac * lax.rsqrt(cv + LN_EPS) * clng_ref[...] + clnb_ref[...]
    o_c = _dot((hf * jax.nn.sigmoid(hf)).astype(BF16), woc_ref[...])

    d = x_ref.shape[1]
    sg = sg_ref[...].astype(F32)
    mix = sg[:, :d] * o_r + sg[:, d:] * o_c
    h1 = x_ref[...] + _dot(mix.astype(BF16), wout_ref[...])
    h1_out[...] = h1
    c = _rms(h1, nfg_ref[...])
    ct_out[...] = c.T.astype(BF16)
    q_out[...] = _dot(c.astype(BF16), wq_ref[...])


def _mix_out(x, y, g, bonus, hg, sg, seg, consts, tm, seq):
    n, d = x.shape
    row = lambda w: pl.BlockSpec((tm, w), lambda i: (i, 0))
    full = lambda a: pl.BlockSpec(a.shape, lambda i: (0,) * a.ndim)
    return pl.pallas_call(
        functools.partial(_mix_out_kernel, seq // tm),
        grid=(n // tm,),
        in_specs=[row(d), row(RWKV_WIDTH), row(RWKV_WIDTH), row(RWKV_WIDTH), row(CONV_WIDTH), row(2 * d),
                  full(seg)] + [full(c) for c in consts],
        out_specs=[row(d), pl.BlockSpec((d, tm), lambda i: (0, i)), row(d)],
        out_shape=[jax.ShapeDtypeStruct((n, d), F32), jax.ShapeDtypeStruct((d, n), BF16),
                   jax.ShapeDtypeStruct((n, d), F32)],
        scratch_shapes=[pltpu.VMEM((tm + CONV_HALO + SUBLANES, CONV_WIDTH), F32)],
        compiler_params=_cparams(("arbitrary",)),
        name="mix_out",
    )(x, y, g, bonus, hg, sg, seg, *consts)


def _sort16_pairs():
    n, pairs, p = PEER_TOPK, [], 1
    while p < n:
        k = p
        while k >= 1:
            for j in range(k % p, n - k, 2 * k):
                for i in range(min(k, n - j - k)):
                    if (i + j) // (2 * p) == (i + j + k) // (2 * p):
                        pairs.append((i + j, i + j + k))
            k //= 2
        p *= 2
    return pairs


_SORT16 = _sort16_pairs()
_BITONIC16 = [(i, i + dd) for dd in (8, 4, 2, 1) for i in range(PEER_TOPK) if not i & dd]
_CELLS = [(i, j) for i in range(PEER_TOPK) for j in range(PEER_TOPK) if (i + 1) * (j + 1) <= PEER_TOPK]


def _exchange(vals, pairs):
    for a, b in pairs:
        hi, lo = jnp.maximum(vals[a], vals[b]), jnp.minimum(vals[a], vals[b])
        vals[a], vals[b] = hi, lo
    return vals


def _top16_sorted(slabs):
    groups = []
    for s in range(0, len(slabs), PEER_TOPK):
        groups.append(_exchange(list(slabs[s:s + PEER_TOPK]), _SORT16))
    while len(groups) > 1:
        merged = []
        for a, b in zip(groups[::2], groups[1::2]):
            c = [jnp.maximum(a[i], b[PEER_TOPK - 1 - i]) for i in range(PEER_TOPK)]
            merged.append(_exchange(c, _BITONIC16))
        groups = merged
    return groups[0]


def _peer_route_kernel(q_ref, ska0_ref, ska1_ref, skb1_ref, n_out, e1_out, r2_out, e2_out, s1_ref, s2_ref, v2_ref):
    qh = q_ref[...].astype(BF16)
    half = qh.shape[1] // 2
    s1_ref[...] = _dot_nt(ska0_ref[...], qh[:, :half])
    s2_ref[...] = _dot_nt(ska1_ref[...], qh[:, half:])
    nh = PEER_HEADS
    slab = lambda ref, n: ref[n * nh:(n + 1) * nh, :]

    v1 = _top16_sorted([slab(s1_ref, n) for n in range(N_KEYS)])
    v2 = _top16_sorted([slab(s2_ref, n) for n in range(N_KEYS)])

    cand = {c: v1[c[0]] + v2[c[1]] for c in _CELLS}
    cnt = {c: jnp.full_like(v1[0], float((c[0] + 1) * (c[1] + 1) - 1)) for c in _CELLS}
    for c1 in _CELLS:
        for c2 in _CELLS:
            if c1[0] < c2[0] and c1[1] > c2[1]:
                first = cand[c1] >= cand[c2]
                cnt[c2] = cnt[c2] + jnp.where(first, 1.0, 0.0)
                cnt[c1] = cnt[c1] + jnp.where(first, 0.0, 1.0)
    top = cand[(0, 0)]
    z = jnp.zeros_like(top)
    taken = [jnp.zeros_like(top) for _ in range(PEER_TOPK)]
    for c in _CELLS:
        sel = cnt[c] < float(PEER_TOPK)
        z = z + jnp.where(sel, jnp.exp(cand[c] - top), 0.0)
        taken[c[0]] = taken[c[0]] + jnp.where(sel, 1.0, 0.0)
    inv_z = 1.0 / z

    for n in range(N_KEYS):
        s = slab(s1_ref, n)
        cnt_a = jnp.zeros_like(s)
        for i in reversed(range(PEER_TOPK)):
            cnt_a = jnp.where(s == v1[i], taken[i], cnt_a)
        n_out[n * nh:(n + 1) * nh, :] = cnt_a
        e1_out[n * nh:(n + 1) * nh, :] = jnp.exp(s - v1[0]) * inv_z

    for j in range(PEER_TOPK):
        v2_ref[j] = v2[j]
    s2b = _dot_nt(skb1_ref[...], qh[:, half:])
    for h in range(nh):
        s = s2b[h * N_KEYS:(h + 1) * N_KEYS, :]
        rank = jnp.zeros_like(s)
        for j in range(PEER_TOPK):
            rank = jnp.where(v2_ref[j, h:h + 1, :] > s, float(j + 1), rank)
        r2_out[h * N_KEYS:(h + 1) * N_KEYS, :] = rank.astype(BF16)
        e2_out[h * N_KEYS:(h + 1) * N_KEYS, :] = jnp.exp(s - v2_ref[0, h:h + 1, :]).astype(BF16)


def _peer_route(q, ska0, ska1, skb1, tn):
    n, d = q.shape
    rows = PEER_HEADS * N_KEYS
    full = lambda a: pl.BlockSpec(a.shape, lambda i: (0,) * a.ndim)
    col = pl.BlockSpec((rows, tn), lambda i: (0, i))
    return pl.pallas_call(
        _peer_route_kernel,
        grid=(n // tn,),
        in_specs=[pl.BlockSpec((tn, d), lambda i: (i, 0)), full(ska0), full(ska1), full(skb1)],
        out_specs=[col] * 4,
        out_shape=[jax.ShapeDtypeStruct((rows, n), F32), jax.ShapeDtypeStruct((rows, n), F32),
                   jax.ShapeDtypeStruct((rows, n), BF16), jax.ShapeDtypeStruct((rows, n), BF16)],
        scratch_shapes=[pltpu.VMEM((rows, tn), F32), pltpu.VMEM((rows, tn), F32),
                        pltpu.VMEM((PEER_TOPK, PEER_HEADS, tn), F32)],
        compiler_params=_cparams(("arbitrary",)),
        name="peer_route",
    )(q, ska0, ska1, skb1)


def _peer_dense_kernel(n_eb, ct_ref, u_ref, vt_ref, n_ref, e1_ref, r2_ref, e2_ref, out_ref,
                       ht0_ref, ht1_ref, cf0_ref, cf1_ref, nb_ref, eb_ref):
    s = pl.program_id(0)

    @pl.when(s == 0)
    def _():
        for ref in (ht0_ref, ht1_ref, cf0_ref, cf1_ref):
            ref[...] = jnp.zeros_like(ref)

    @pl.when((s <= 2) | ((s - 2) % n_eb == 0))
    def _():
        out_ref[...] = jnp.zeros_like(out_ref)

    nh = PEER_HEADS
    tn = ct_ref.shape[1]
    rg = BF16_ROWS
    a_blk = u_ref.shape[0] // N_KEYS
    zero = jnp.zeros((rg, tn), BF16)
    g_half = N_KEYS // rg // 2

    def step(ht_new, ht_old, cf_old, cf_new):
        keys = 4
        n_chunk = a_blk // keys
        e_rows, d_rows = u_ref.shape[0] // n_chunk, out_ref.shape[0] // n_chunk
        for ch in range(n_chunk):
            rows = slice(ch * e_rows, (ch + 1) * e_rows)
            ht_new[rows, :] = _dot(u_ref[rows, :], ct_ref[...])
            for r in range(keys * ch * nh, keys * (ch + 1) * nh):
                for cols in [slice(lt * LANES, (lt + 1) * LANES) for lt in range(tn // LANES)]:
                    for src, dst in ((n_ref, nb_ref), (e1_ref, eb_ref)):
                        dst[r * rg:(r + 1) * rg, cols] = jnp.broadcast_to(src[r:r + 1, cols], (rg, LANES)).astype(BF16)
            for al, g0 in [(a, g) for a in range(keys * ch, keys * (ch + 1)) for g in (0, g_half)]:
                gate = [zero] * g_half
                for h in range(nh):
                    r = al * nh + h
                    n_a, e1_a = nb_ref[r * rg:(r + 1) * rg, :], eb_ref[r * rg:(r + 1) * rg, :]
                    for g in range(g_half):
                        rows = slice(h * N_KEYS + (g0 + g) * rg, h * N_KEYS + (g0 + g + 1) * rg)
                        gate[g] = gate[g] + jnp.where(r2_ref[rows, :] < n_a, e2_ref[rows, :], zero) * e1_a
                for g in range(g_half):
                    rows = slice(al * N_KEYS + (g0 + g) * rg, al * N_KEYS + (g0 + g + 1) * rg)
                    ht = ht_old[rows, :].astype(BF16)
                    act = ht * (0.5 * (1.0 + lax.erf(ht * (2.0 ** -0.5))))
                    cf_old[rows, :] = act * gate[g]
            rows = slice(ch * d_rows, (ch + 1) * d_rows)
            out_ref[rows, :] += _dot(vt_ref[rows, :], cf_new[...])

    @pl.when(s % 2 == 0)
    def _():
        step(ht0_ref, ht1_ref, cf1_ref, cf0_ref)

    @pl.when(s % 2 == 1)
    def _():
        step(ht1_ref, ht0_ref, cf0_ref, cf1_ref)


def _peer_dense(ct, u_bf, vt_bf, n_a, e1, r2, e2, tn, a_blk):
    d, n = ct.shape
    e_blk = a_blk * N_KEYS
    n_eb = u_bf.shape[0] // e_blk
    last = (n // tn) * n_eb - 1
    rows = PEER_HEADS * N_KEYS
    pair = lambda s, lag: jnp.clip(s - lag, 0, last)
    tok = lambda s, lag: pair(s, lag) // n_eb
    blk = lambda s, lag: pair(s, lag) % n_eb
    return pl.pallas_call(
        functools.partial(_peer_dense_kernel, n_eb),
        grid=(last + 3,),
        in_specs=[pl.BlockSpec((d, tn), lambda s: (0, tok(s, 0))),
                  pl.BlockSpec((e_blk, d), lambda s: (blk(s, 0), 0)),
                  pl.BlockSpec((d, e_blk), lambda s: (0, blk(s, 2))),
                  pl.BlockSpec((a_blk * PEER_HEADS, tn), lambda s: (blk(s, 1), tok(s, 1))),
                  pl.BlockSpec((a_blk * PEER_HEADS, tn), lambda s: (blk(s, 1), tok(s, 1))),
                  pl.BlockSpec((rows, tn), lambda s: (0, tok(s, 1))),
                  pl.BlockSpec((rows, tn), lambda s: (0, tok(s, 1)))],
        out_specs=pl.BlockSpec((d, tn), lambda s: (0, tok(s, 2))),
        out_shape=jax.ShapeDtypeStruct((d, n), F32),
        scratch_shapes=([pltpu.VMEM((e_blk, tn), F32)] * 2 + [pltpu.VMEM((e_blk, tn), BF16)] * 2
                        + [pltpu.VMEM((a_blk * PEER_HEADS * BF16_ROWS, tn), BF16)] * 2),
        compiler_params=_cparams(("arbitrary",)),
        name="peer_dense",
    )(ct, u_bf, vt_bf, n_a, e1, r2, e2)


def _final_kernel(last_layer, h1_ref, ffnt_ref, p_ref, png_ref, wg_ref, wp_ref, fng_ref, out_ref):
    h2 = h1_ref[...] + ffnt_ref[...].T
    gate = jax.nn.sigmoid(_dot(_rms(h2, png_ref[...]).astype(BF16), wg_ref[...]))
    h3 = h2 + gate * _dot(p_ref[...].astype(BF16), wp_ref[...])
    out_ref[...] = _rms(h3, fng_ref[...]) if last_layer else h3


def _final(h1, ffnt, p, png, wg, wp, fng, tm, last_layer):
    n, d = h1.shape
    full = lambda a: pl.BlockSpec(a.shape, lambda i: (0,) * a.ndim)
    return pl.pallas_call(
        functools.partial(_final_kernel, last_layer),
        grid=(n // tm,),
        in_specs=[pl.BlockSpec((tm, d), lambda i: (i, 0)), pl.BlockSpec((d, tm), lambda i: (0, i)),
                  pl.BlockSpec((tm, p.shape[1]), lambda i: (i, 0)), full(png), full(wg), full(wp), full(fng)],
        out_specs=pl.BlockSpec((tm, d), lambda i: (i, 0)),
        out_shape=jax.ShapeDtypeStruct((n, d), F32),
        compiler_params=_cparams(("arbitrary",)),
        name="final",
    )(h1, ffnt, p, png, wg, wp, fng)


def _tile(n, want):
    t = min(n, want)
    assert n % t == 0, (n, t)
    return t


def _tiles(seq, n):
    return dict(in_proj=_tile(seq, 512), prep=_tile(seq, 512), scan_steps=_tile(seq, 64), mix=_tile(seq, 512),
                route=_tile(n, 512), dense_tokens=_tile(n, 1024), dense_keys=8, final=_tile(seq, 1024))


def kernel(x, p, norm_mix_g, w_in, rwkv_mu, rwkv_w0, rwkv_w_up, rwkv_a0, rwkv_a_up, rwkv_g_up, rwkv_k_k, rwkv_k_a, rwkv_r_k, rwkv_lnx_g, rwkv_lnx_b, rwkv_w_o, conv_dw_w, conv_dw_b, conv_ln_g, conv_ln_b, conv_w_o, w_out, norm_ffn_g, peer_w_q, peer_sub_keys, peer_u, peer_v, ple_w_proj, ple_norm_g, ple_w_gate, final_norm_g):
    bsz, seq, d = x.shape
    n = bsz * seq
    nh, hd = RWKV_HEADS, RWKV_HEAD_DIM
    row = lambda a: a.reshape(1, -1)
    head_of = np.arange(RWKV_WIDTH) // hd
    seg = jnp.asarray(head_of[:, None] == head_of[None, :], BF16)
    h = x.reshape(n, d)
    tiles = _tiles(seq, n)

    for l in range(w_in.shape[0]):
        zr, hg, sg = _in_proj(h, row(norm_mix_g[l]), w_in[l].astype(BF16), tiles["in_proj"])
        r, w, k, v, nkk, kka, g, bonus = _rwkv_prep(
            zr, row(rwkv_mu[l]), row(rwkv_w0[l]), rwkv_w_up[l].astype(BF16), row(rwkv_a0[l]),
            rwkv_a_up[l].astype(BF16), rwkv_g_up[l].astype(BF16),
            row(rwkv_k_k[l]), row(rwkv_k_a[l]), row(rwkv_r_k[l]), seg, tiles["prep"], seq)

        seqs = lambda a: a.reshape(bsz, seq, RWKV_WIDTH)
        y = _rwkv_scan(seqs(r), seqs(w), seqs(k), seqs(v), seqs(kka), seqs(nkk), tiles["scan_steps"])
        y = y.reshape(n, RWKV_WIDTH)

        consts = (row(rwkv_lnx_g[l]), row(rwkv_lnx_b[l]), rwkv_w_o[l].astype(BF16), conv_dw_w[l],
                  row(conv_dw_b[l]), row(conv_ln_g[l]), row(conv_ln_b[l]), conv_w_o[l].astype(BF16),
                  w_out[l].astype(BF16), row(norm_ffn_g[l]),
                  peer_w_q[l].reshape(d, PEER_HEADS, 2, PEER_HALF).transpose(0, 2, 1, 3).reshape(d, -1).astype(BF16))
        h1, ct, q = _mix_out(h, y, g, bonus, hg, sg, seg, consts, tiles["mix"], seq)

        sk = peer_sub_keys[l]
        eye = jnp.eye(PEER_HEADS, dtype=sk.dtype)
        blk = lambda half, order: jnp.einsum("hnd,hg->" + order, sk[:, half], eye).reshape(
            PEER_HEADS * N_KEYS, PEER_HEADS * PEER_HALF).astype(BF16)
        n_a, e1, r2, e2 = _peer_route(q, blk(0, "nhgd"), blk(1, "nhgd"), blk(1, "hngd"), tiles["route"])
        ffnt = _peer_dense(ct, peer_u[l].astype(BF16), peer_v[l].T.astype(BF16), n_a, e1, r2, e2,
                           tiles["dense_tokens"], tiles["dense_keys"])
        h = _final(h1, ffnt, p[l].reshape(n, -1), row(ple_norm_g[l]), ple_w_gate[l].astype(BF16),
                   ple_w_proj[l].astype(BF16), row(final_norm_g), tiles["final"], l == w_in.shape[0] - 1)
    return h.reshape(bsz, seq, d)
```
